```python
import math
import jax, jax.numpy as jnp
from jax import lax
import numpy as np

D_MODEL = 1024
BATCH = 32
SEQ = 2048
DEPTH = 1

CHUNK = 64
MIX_WIDTH = D_MODEL
SSM_WIDTH = MIX_WIDTH // 2
POOL_WIDTH = MIX_WIDTH - SSM_WIDTH
SSM_GROUP = 16
SSM_GROUPS = SSM_WIDTH // SSM_GROUP
SSM_STATE = 64
POOL_WINDOWS = (2, 4, 8, 16)
POOL_GROUPS = len(POOL_WINDOWS)
POOL_GROUP_WIDTH = POOL_WIDTH // POOL_GROUPS
D_FF = ((8 * D_MODEL // 3 + 255) // 256) * 256
CONV_WIDTH = 3
N_MOD = 6
EPS = 1e-6
DT_MIN = 1e-3
DT_MAX = 1e-1

kernel_name = "hymba_s5_pool_convffn_block"


def rms_norm(x, g):
    xf = x.astype(jnp.float32)
    y = xf * lax.rsqrt(jnp.mean(xf * xf, axis=-1, keepdims=True) + EPS)
    return (y * g.astype(jnp.float32)).astype(x.dtype)


def modulate(h, shift, scale):
    return h * (1 + scale[:, None, :]) + shift[:, None, :]


def s5_mixer(u, lam_re, lam_im, log_dt, b_re, b_im, c_re, c_im, d_skip):
    f32 = jnp.float32
    bsz, seq, _ = u.shape
    n_chunks = seq // CHUNK
    lam = lax.complex(lam_re.astype(f32), lam_im.astype(f32))
    dt = jnp.exp(log_dt.astype(f32))[:, None]
    lam_bar = jnp.exp(lam * dt)
    b_mat = lax.complex(b_re.astype(f32), b_im.astype(f32))
    b_bar = ((lam_bar - 1) / lam)[..., None] * b_mat
    c_mat = lax.complex(c_re.astype(f32), c_im.astype(f32))
    d = d_skip.astype(f32).reshape(SSM_GROUPS, SSM_GROUP)
    u_chunks = u.astype(f32).reshape(bsz, n_chunks, CHUNK, SSM_GROUPS, SSM_GROUP)
    u_chunks = u_chunks.transpose(1, 0, 2, 3, 4)

    def combine(e1, e2):
        a1, b1 = e1
        a2, b2 = e2
        return a1 * a2, a2 * b1 + b2

    def chunk_step(state, u_c):
        bu = jnp.einsum('bcgh,gph->bcgp', u_c.astype(jnp.complex64), b_bar)
        bu = bu.at[:, 0].add(lam_bar * state)
        a = jnp.broadcast_to(lam_bar, bu.shape)
        _, xs = lax.associative_scan(combine, (a, bu), axis=1)
        y = jnp.einsum('bcgp,ghp->bcgh', xs, c_mat).real + d * u_c
        return xs[:, -1], y

    state0 = jnp.zeros((bsz, SSM_GROUPS, SSM_STATE), jnp.complex64)
    _, ys = lax.scan(chunk_step, state0, u_chunks)
    return ys.transpose(1, 0, 2, 3, 4).reshape(bsz, seq, SSM_WIDTH)


def multiscale_pool(p, w_pool, b_pool, pool_scale):
    f32 = jnp.float32
    bsz, seq, _ = p.shape
    pf = p.astype(f32)
    cs = jnp.concatenate([jnp.zeros((bsz, 1, POOL_WIDTH), f32), jnp.cumsum(pf, axis=1)], axis=1)
    pos = jnp.arange(1, seq + 1, dtype=f32)[None, :, None]
    pooled = []
    for gi, w in enumerate(POOL_WINDOWS):
        sl = slice(gi * POOL_GROUP_WIDTH, (gi + 1) * POOL_GROUP_WIDTH)
        cs_g = cs[:, :, sl]
        lagged = jnp.concatenate(
            [jnp.zeros((bsz, w - 1, POOL_GROUP_WIDTH), f32), cs_g[:, :seq + 1 - w]], axis=1)
        mean = (cs_g[:, 1:] - lagged) / jnp.minimum(pos, float(w))
        pooled.append(mean - pf[:, :, sl])
    z = jnp.stack(pooled, axis=2)
    z = jnp.einsum('blgc,gcd->blgd', z, w_pool) + b_pool
    return z.reshape(bsz, seq, POOL_WIDTH) * pool_scale


def causal_dwconv(h, w, b):
    ch = h.shape[-1]
    y = lax.conv_general_dilated(
        h, w[:, None, :].astype(h.dtype), window_strides=(1,),
        padding=[(CONV_WIDTH - 1, 0)], dimension_numbers=('NWC', 'WIO', 'NWC'),
        feature_group_count=ch)
    return y + b


def hybrid_layer(x, c, w_ada, b_ada, g_norm_mix, w_in, ssm_lam_re, ssm_lam_im, ssm_log_dt,
                 ssm_b_re, ssm_b_im, ssm_c_re, ssm_c_im, ssm_d, w_glu, b_glu, w_pool, b_pool,
                 pool_scale, w_out, g_norm_ffn, w_up, w_conv, b_conv, w_down):
    mod = jax.nn.silu(c) @ w_ada + b_ada
    sh1, sc1, gt1, sh2, sc2, gt2 = jnp.split(mod, N_MOD, axis=-1)

    h = modulate(rms_norm(x, g_norm_mix), sh1, sc1)
    z_in = h @ w_in
    u, p = z_in[..., :SSM_WIDTH], z_in[..., SSM_WIDTH:]
    y = jax.nn.gelu(s5_mixer(u, ssm_lam_re, ssm_lam_im, ssm_log_dt, ssm_b_re, ssm_b_im,
                             ssm_c_re, ssm_c_im, ssm_d))
    val, gate = jnp.split(y @ w_glu + b_glu, 2, axis=-1)
    y_ssm = val * jax.nn.sigmoid(gate)
    y_pool = multiscale_pool(p, w_pool, b_pool, pool_scale)
    mixed = jnp.concatenate([y_ssm, y_pool], axis=-1) @ w_out
    x = x + gt1[:, None, :] * mixed

    h = modulate(rms_norm(x, g_norm_ffn), sh2, sc2)
    v, g = jnp.split(h @ w_up, 2, axis=-1)
    g = causal_dwconv(g, w_conv, b_conv)
    x = x + gt2[:, None, :] * ((jax.nn.silu(g) * v) @ w_down)
    return x


def setup_inputs(seed: int = 0) -> dict:
    key = jax.random.key(seed)
    ks = jax.random.split(key, 32)

    def nrm(k, shape, scale):
        return jax.random.normal(k, shape, jnp.float32) * scale

    L = DEPTH
    G, P, H = SSM_GROUPS, SSM_STATE, SSM_GROUP
    n_idx = jnp.arange(P, dtype=jnp.float32)
    return {
        "x": nrm(ks[0], (BATCH, SEQ, D_MODEL), 1.0),
        "c": nrm(ks[1], (BATCH, D_MODEL), 1.0),
        "w_ada": nrm(ks[2], (L, D_MODEL, N_MOD * D_MODEL), D_MODEL ** -0.5),
        "b_ada": nrm(ks[3], (L, N_MOD * D_MODEL), 0.02),
        "g_norm_mix": 1.0 + nrm(ks[4], (L, D_MODEL), 0.02),
        "w_in": nrm(ks[5], (L, D_MODEL, MIX_WIDTH), D_MODEL ** -0.5),
        "ssm_lam_re": -0.5 + nrm(ks[6], (L, G, P), 0.01),
        "ssm_lam_im": math.pi * n_idx + nrm(ks[7], (L, G, P), 0.01),
        "ssm_log_dt": jax.random.uniform(ks[8], (L, G), jnp.float32,
                                         math.log(DT_MIN), math.log(DT_MAX)),
        "ssm_b_re": nrm(ks[9], (L, G, P, H), (2 * H) ** -0.5),
        "ssm_b_im": nrm(ks[10], (L, G, P, H), (2 * H) ** -0.5),
        "ssm_c_re": nrm(ks[11], (L, G, H, P), (2 * P) ** -0.5 * 4.0),
        "ssm_c_im": nrm(ks[12], (L, G, H, P), (2 * P) ** -0.5 * 4.0),
        "ssm_d": nrm(ks[13], (L, SSM_WIDTH), 1.0),
        "w_glu": nrm(ks[14], (L, SSM_WIDTH, 2 * SSM_WIDTH), SSM_WIDTH ** -0.5),
        "b_glu": nrm(ks[15], (L, 2 * SSM_WIDTH), 0.02),
        "w_pool": nrm(ks[16], (L, POOL_GROUPS, POOL_GROUP_WIDTH, POOL_GROUP_WIDTH),
                      POOL_GROUP_WIDTH ** -0.5),
        "b_pool": nrm(ks[17], (L, POOL_GROUPS, POOL_GROUP_WIDTH), 0.02),
        "pool_scale": 1.0 + nrm(ks[18], (L, POOL_WIDTH), 0.1),
        "w_out": nrm(ks[19], (L, MIX_WIDTH, D_MODEL), MIX_WIDTH ** -0.5),
        "g_norm_ffn": 1.0 + nrm(ks[20], (L, D_MODEL), 0.02),
        "w_up": nrm(ks[21], (L, D_MODEL, 2 * D_FF), D_MODEL ** -0.5),
        "w_conv": nrm(ks[22], (L, CONV_WIDTH, D_FF), CONV_WIDTH ** -0.5),
        "b_conv": nrm(ks[23], (L, D_FF), 0.02),
        "w_down": nrm(ks[24], (L, D_FF, D_MODEL), D_FF ** -0.5),
        "g_norm_final": 1.0 + nrm(ks[25], (D_MODEL,), 0.02),
    }


def reference(x, c, w_ada, b_ada, g_norm_mix, w_in, ssm_lam_re, ssm_lam_im, ssm_log_dt,
              ssm_b_re, ssm_b_im, ssm_c_re, ssm_c_im, ssm_d, w_glu, b_glu, w_pool, b_pool,
              pool_scale, w_out, g_norm_ffn, w_up, w_conv, b_conv, w_down, g_norm_final):
    for l in range(DEPTH):
        x = hybrid_layer(x, c, w_ada[l], b_ada[l], g_norm_mix[l], w_in[l], ssm_lam_re[l],
                         ssm_lam_im[l], ssm_log_dt[l], ssm_b_re[l], ssm_b_im[l], ssm_c_re[l],
                         ssm_c_im[l], ssm_d[l], w_glu[l], b_glu[l], w_pool[l], b_pool[l],
                         pool_scale[l], w_out[l], g_norm_ffn[l], w_up[l], w_conv[l],
                         b_conv[l], w_down[l])
    return rms_norm(x, g_norm_final)
```

```python
import functools

import jax
import jax.numpy as jnp
from jax import lax
from jax.experimental import pallas as pl
from jax.experimental.pallas import tpu as pltpu

V7X_LANES = 128
V7X_SUBLANES = 8
V7X_VMEM_LIMIT_BYTES = 56 * 1024 * 1024

BATCH_GROUP = V7X_SUBLANES

SSM_GROUP = 16
SSM_STATE = 64
POOL_WINDOWS = (2, 4, 8, 16)
POOL_HALO = 16
CONV_WIDTH = 3
CONV_HALO = 8
N_MOD = 6
EPS = 1e-6

F32 = jnp.float32
BF16 = jnp.bfloat16


def _const_spec(shape):
    zeros = (0,) * len(shape)
    return pl.BlockSpec(shape, lambda *_: zeros, pipeline_mode=pl.Buffered(1))


def _rms_norm(x, g):
    ms = jnp.mean(x * x, axis=-1, keepdims=True)
    return x * lax.rsqrt(ms + EPS) * g


def _adaln_kernel(c_ref, w_ref, b_ref, o_ref):
    c = c_ref[...]
    s = c * jax.nn.sigmoid(c)
    o_ref[...] = jnp.dot(s.astype(BF16), w_ref[...].astype(BF16),
                         preferred_element_type=F32) + b_ref[...]


def _adaln(c, w_ada, b_ada):
    bsz, d = c.shape
    n = w_ada.shape[1]
    return pl.pallas_call(
        _adaln_kernel,
        grid=(n // d,),
        in_specs=[
            pl.BlockSpec((bsz, d), lambda k: (0, 0)),
            pl.BlockSpec((d, d), lambda k: (0, k)),
            pl.BlockSpec((1, d), lambda k: (0, k)),
        ],
        out_specs=pl.BlockSpec((bsz, d), lambda k: (0, k)),
        out_shape=jax.ShapeDtypeStruct((bsz, n), F32),
        name="adaln",
    )(c, w_ada, b_ada.reshape(1, n))


def _mix_in_kernel(x_ref, sh_ref, sc_ref, g_ref, w_ref, u_ref, p_ref, *, tm, ssm_w):
    nb, _, d = x_ref.shape
    h = _rms_norm(x_ref[...], g_ref[...])
    h = h * (1.0 + sc_ref[...][:, None, :]) + sh_ref[...][:, None, :]
    z = jnp.dot(h.reshape(nb * tm, d).astype(BF16), w_ref[...], preferred_element_type=F32)
    p_ref[...] = z[:, ssm_w:].reshape(p_ref.shape)
    for b in range(nb):
        for j in range(ssm_w // V7X_LANES):
            u_ref[0, j, pl.ds(b, tm, stride=nb), :] = (
                z[b * tm:(b + 1) * tm, j * V7X_LANES:(j + 1) * V7X_LANES])


def _mix_in(x, mod, g_norm, w_in_bf, *, tm, ssm_w):
    bsz, seq, d = x.shape
    mix_w = w_in_bf.shape[1]
    pool_w = mix_w - ssm_w
    nb = BATCH_GROUP
    n_slab = ssm_w // V7X_LANES
    kern = functools.partial(_mix_in_kernel, tm=tm, ssm_w=ssm_w)
    return pl.pallas_call(
        kern,
        grid=(bsz // nb, seq // tm),
        in_specs=[
            pl.BlockSpec((nb, tm, d), lambda g, i: (g, i, 0)),
            pl.BlockSpec((nb, d), lambda g, i: (g, 0)),
            pl.BlockSpec((nb, d), lambda g, i: (g, 1)),
            _const_spec((1, d)),
            _const_spec((d, mix_w)),
        ],
        out_specs=[
            pl.BlockSpec((1, n_slab, tm * nb, V7X_LANES), lambda g, i: (g, 0, i, 0)),
            pl.BlockSpec((nb, tm, pool_w), lambda g, i: (g, i, 0)),
        ],
        out_shape=[
            jax.ShapeDtypeStruct((bsz // nb, n_slab, seq * nb, V7X_LANES), F32),
            jax.ShapeDtypeStruct((bsz, seq, pool_w), F32),
        ],
        compiler_params=pltpu.CompilerParams(
            dimension_semantics=("arbitrary", "arbitrary"),
            vmem_limit_bytes=V7X_VMEM_LIMIT_BYTES),
        name="mix_in",
    )(x, mod, mod, g_norm.reshape(1, d), w_in_bf)


def _s5_kernel(u_ref, bmat_ref, cmat_ref, lam_ref, d_ref, y_ref, bu_ref, state_ref, *, tt):
    n_slab = u_ref.shape[1]
    half = bu_ref.shape[2] // 2
    n_q = half // V7X_LANES
    nb = BATCH_GROUP

    @pl.when(pl.program_id(1) == 0)
    def _():
        state_ref[...] = jnp.zeros_like(state_ref)

    for j in range(n_slab):
        bu_ref[j] = jnp.dot(u_ref[0, j].astype(BF16), bmat_ref[j], preferred_element_type=F32)

    for j0 in range(0, n_slab, 2):
        chains = [(j, q) for j in range(j0, min(j0 + 2, n_slab)) for q in range(n_q)]
        lam = [(lam_ref[j, 0, :, q * V7X_LANES:(q + 1) * V7X_LANES],
                lam_ref[j, 1, :, q * V7X_LANES:(q + 1) * V7X_LANES]) for j, q in chains]
        init = tuple(
            (state_ref[j, :, q * V7X_LANES:(q + 1) * V7X_LANES],
             state_ref[j, :, half + q * V7X_LANES:half + (q + 1) * V7X_LANES]) for j, q in chains)

        def body(t, carry, chains=chains, lam=lam):
            r0 = pl.multiple_of(t * nb, nb)
            out = []
            for (j, q), (a_re, a_im), (s_re, s_im) in zip(chains, lam, carry):
                c_re = slice(q * V7X_LANES, (q + 1) * V7X_LANES)
                c_im = slice(half + q * V7X_LANES, half + (q + 1) * V7X_LANES)
                n_re = a_re * s_re - a_im * s_im + bu_ref[j, pl.ds(r0, nb), c_re]
                n_im = a_re * s_im + a_im * s_re + bu_ref[j, pl.ds(r0, nb), c_im]
                bu_ref[j, pl.ds(r0, nb), c_re] = n_re
                bu_ref[j, pl.ds(r0, nb), c_im] = n_im
                out.append((n_re, n_im))
            return tuple(out)

        final = lax.fori_loop(0, tt, body, init)
        for (j, q), (s_re, s_im) in zip(chains, final):
            state_ref[j, :, q * V7X_LANES:(q + 1) * V7X_LANES] = s_re
            state_ref[j, :, half + q * V7X_LANES:half + (q + 1) * V7X_LANES] = s_im

    for j in range(n_slab):
        y = jnp.dot(bu_ref[j].astype(BF16), cmat_ref[j], preferred_element_type=F32)
        y_ref[0, j] = y + d_ref[j] * u_ref[0, j]


def _s5(u_tm, bmat, cmat, lam, dvec, *, tt):
    n_bg, n_slab, rows, _ = u_tm.shape
    nb = BATCH_GROUP
    seq = rows // nb
    n_state = bmat.shape[2]
    kern = functools.partial(_s5_kernel, tt=tt)
    return pl.pallas_call(
        kern,
        grid=(n_bg, seq // tt),
        in_specs=[
            pl.BlockSpec((1, n_slab, tt * nb, V7X_LANES), lambda g, i: (g, 0, i, 0)),
            _const_spec(bmat.shape),
            _const_spec(cmat.shape),
            _const_spec(lam.shape),
            _const_spec(dvec.shape),
        ],
        out_specs=pl.BlockSpec((1, n_slab, tt * nb, V7X_LANES), lambda g, i: (g, 0, i, 0)),
        out_shape=jax.ShapeDtypeStruct(u_tm.shape, F32),
        scratch_shapes=[
            pltpu.VMEM((n_slab, tt * nb, n_state), F32),
            pltpu.VMEM((n_slab, nb, n_state), F32),
        ],
        compiler_params=pltpu.CompilerParams(
            dimension_semantics=("arbitrary", "arbitrary"),
            vmem_limit_bytes=V7X_VMEM_LIMIT_BYTES),
        name="s5_scan",
    )(u_tm, bmat, cmat, lam, dvec)


def _gelu_tanh(x):
    return 0.5 * x * (1.0 + jnp.tanh(0.7978845608028654 * (x + 0.044715 * (x * x * x))))


def _mix_out_kernel(x_ref, p_ref, y_ref, gt_ref, wglu_ref, bglu_ref, wpool_ref, bpool_ref,
                    pscale_ref, wout_ref, o_ref, ynat_ref, pbuf_ref, *, tm):
    nb, _, d = x_ref.shape
    ssm_w = ynat_ref.shape[1]
    pool_w = p_ref.shape[2]
    pgw = pool_w // len(POOL_WINDOWS)
    i = pl.program_id(1)

    for b in range(nb):
        for j in range(ssm_w // V7X_LANES):
            ynat_ref[b * tm:(b + 1) * tm, j * V7X_LANES:(j + 1) * V7X_LANES] = (
                y_ref[0, j, pl.ds(b, tm, stride=nb), :])
    yg = _gelu_tanh(ynat_ref[...]).astype(BF16)
    t = jnp.dot(yg, wglu_ref[...], preferred_element_type=F32) + bglu_ref[...]
    y_ssm = t[:, :ssm_w] * jax.nn.sigmoid(t[:, ssm_w:])

    @pl.when(i == 0)
    def _():
        pbuf_ref[:, 0:POOL_HALO, :] = jnp.zeros((nb, POOL_HALO, pool_w), F32)

    @pl.when(i > 0)
    def _():
        pbuf_ref[:, 0:POOL_HALO, :] = pbuf_ref[:, tm:tm + POOL_HALO, :]

    pbuf_ref[:, POOL_HALO:, :] = p_ref[...]
    pos = (i * tm + 1 + lax.broadcasted_iota(jnp.int32, (1, tm, 1), 1)).astype(F32)
    y_pool = []
    for gi, w in enumerate(POOL_WINDOWS):
        cols = slice(gi * pgw, (gi + 1) * pgw)
        tok = pbuf_ref[:, POOL_HALO:POOL_HALO + tm, cols]
        acc = tok
        for k in range(1, w):
            acc = acc + pbuf_ref[:, POOL_HALO - k:POOL_HALO - k + tm, cols]
        z = acc / jnp.minimum(pos, float(w)) - tok
        zp = jnp.dot(z.reshape(nb * tm, pgw).astype(BF16), wpool_ref[gi],
                     preferred_element_type=F32)
        y_pool.append((zp + bpool_ref[:, cols]) * pscale_ref[:, cols])
    y_pool = jnp.concatenate(y_pool, axis=-1)

    mixed = jnp.dot(y_ssm.astype(BF16), wout_ref[:ssm_w, :], preferred_element_type=F32)
    mixed = mixed + jnp.dot(y_pool.astype(BF16), wout_ref[ssm_w:, :], preferred_element_type=F32)
    o_ref[...] = x_ref[...] + gt_ref[...][:, None, :] * mixed.reshape(nb, tm, d)


def _mix_out(x, p, y_tm, mod, w_glu_bf, b_glu, w_pool_bf, b_pool, pool_scale, w_out_bf, *, tm):
    bsz, seq, d = x.shape
    pool_w = p.shape[2]
    n_bg, n_slab, _, _ = y_tm.shape
    ssm_w = n_slab * V7X_LANES
    nb = BATCH_GROUP
    kern = functools.partial(_mix_out_kernel, tm=tm)
    return pl.pallas_call(
        kern,
        grid=(n_bg, seq // tm),
        in_specs=[
            pl.BlockSpec((nb, tm, d), lambda g, i: (g, i, 0)),
            pl.BlockSpec((nb, tm, pool_w), lambda g, i: (g, i, 0)),
            pl.BlockSpec((1, n_slab, tm * nb, V7X_LANES), lambda g, i: (g, 0, i, 0)),
            pl.BlockSpec((nb, d), lambda g, i: (g, 2)),
            _const_spec(w_glu_bf.shape),
            _const_spec((1, 2 * ssm_w)),
            _const_spec(w_pool_bf.shape),
            _const_spec((1, pool_w)),
            _const_spec((1, pool_w)),
            _const_spec(w_out_bf.shape),
        ],
        out_specs=pl.BlockSpec((nb, tm, d), lambda g, i: (g, i, 0)),
        out_shape=jax.ShapeDtypeStruct(x.shape, F32),
        scratch_shapes=[
            pltpu.VMEM((nb * tm, ssm_w), F32),
            pltpu.VMEM((nb, POOL_HALO + tm, pool_w), F32),
        ],
        compiler_params=pltpu.CompilerParams(
            dimension_semantics=("arbitrary", "arbitrary"),
            vmem_limit_bytes=V7X_VMEM_LIMIT_BYTES),
        name="mix_out",
    )(x, p, y_tm, mod, w_glu_bf, b_glu.reshape(1, -1), w_pool_bf, b_pool.reshape(1, -1),
      pool_scale.reshape(1, -1), w_out_bf)


def _ffn_kernel(x_ref, sh_ref, sc_ref, gt_ref, gn_ref, wup_ref, wconv_ref, bconv_ref, wdown_ref,
                gfin_ref, o_ref, gbuf_ref, gprev_ref, acc_ref, *, tm, fc):
    d = x_ref.shape[2]
    d_ff = wdown_ref.shape[0]
    x = x_ref[0]
    h = _rms_norm(x, gn_ref[...])
    h = (h * (1.0 + sc_ref[0, 0]) + sh_ref[0, 0]).astype(BF16)

    @pl.when(pl.program_id(1) == 0)
    def _():
        gprev_ref[...] = jnp.zeros_like(gprev_ref)

    for c in range(d_ff // fc):
        cols = slice(c * fc, (c + 1) * fc)
        v = jnp.dot(h, wup_ref[:, c * fc:(c + 1) * fc], preferred_element_type=F32)
        g = jnp.dot(h, wup_ref[:, d_ff + c * fc:d_ff + (c + 1) * fc], preferred_element_type=F32)
        gbuf_ref[0:CONV_HALO, :] = gprev_ref[:, cols]
        gbuf_ref[CONV_HALO:, :] = g
        gprev_ref[:, cols] = g[tm - CONV_HALO:, :]
        gc = bconv_ref[:, cols] + wconv_ref[CONV_WIDTH - 1:CONV_WIDTH, cols] * g
        for k in range(1, CONV_WIDTH):
            gc = gc + (wconv_ref[CONV_WIDTH - 1 - k:CONV_WIDTH - k, cols]
                       * gbuf_ref[CONV_HALO - k:CONV_HALO - k + tm, :])
        a = (gc * jax.nn.sigmoid(gc) * v).astype(BF16)
        part = jnp.dot(a, wdown_ref[cols, :], preferred_element_type=F32)
        if c == 0:
            acc_ref[...] = part
        else:
            acc_ref[...] += part

    y = x + gt_ref[0, 0] * acc_ref[...]
    o_ref[0] = _rms_norm(y, gfin_ref[...])


def _ffn(x, mod, g_norm, w_up_bf, w_conv, b_conv, w_down_bf, g_final, *, tm, fc):
    bsz, seq, d = x.shape
    d_ff = w_down_bf.shape[0]
    mod4 = mod.reshape(bsz, N_MOD, 1, d)
    kern = functools.partial(_ffn_kernel, tm=tm, fc=fc)
    return pl.pallas_call(
        kern,
        grid=(bsz, seq // tm),
        in_specs=[
            pl.BlockSpec((1, tm, d), lambda b, i: (b, i, 0)),
            pl.BlockSpec((1, 1, 1, d), lambda b, i: (b, 3, 0, 0)),
            pl.BlockSpec((1, 1, 1, d), lambda b, i: (b, 4, 0, 0)),
            pl.BlockSpec((1, 1, 1, d), lambda b, i: (b, 5, 0, 0)),
            _const_spec((1, d)),
            _const_spec(w_up_bf.shape),
            _const_spec(w_conv.shape),
            _const_spec((1, d_ff)),
            _const_spec(w_down_bf.shape),
            _const_spec((1, d)),
        ],
        out_specs=pl.BlockSpec((1, tm, d), lambda b, i: (b, i, 0)),
        out_shape=jax.ShapeDtypeStruct(x.shape, F32),
        scratch_shapes=[
            pltpu.VMEM((CONV_HALO + tm, fc), F32),
            pltpu.VMEM((CONV_HALO, d_ff), F32),
            pltpu.VMEM((tm, d), F32),
        ],
        compiler_params=pltpu.CompilerParams(
            dimension_semantics=("arbitrary", "arbitrary"),
            vmem_limit_bytes=V7X_VMEM_LIMIT_BYTES),
        name="ffn",
    )(x, mod4, mod4, mod4, g_norm.reshape(1, d), w_up_bf, w_conv, b_conv.reshape(1, d_ff),
      w_down_bf, g_final.reshape(1, d))


def _ssm_params(lam_re, lam_im, log_dt, b_re, b_im, c_re, c_im, d_skip):
    n_groups, n_state = lam_re.shape
    gps = V7X_LANES // SSM_GROUP
    n_slab = n_groups // gps
    half = gps * n_state
    lam = lax.complex(lam_re.astype(F32), lam_im.astype(F32))
    dt = jnp.exp(log_dt.astype(F32))[:, None]
    lam_bar = jnp.exp(lam * dt)
    b_bar = ((lam_bar - 1) / lam)[..., None] * lax.complex(b_re.astype(F32), b_im.astype(F32))
    eye = jnp.eye(gps, dtype=F32)

    def block_diag_in(m):
        m = m.reshape(n_slab, gps, n_state, SSM_GROUP)
        return jnp.einsum('sgph,gk->sghkp', m, eye).reshape(n_slab, V7X_LANES, half)

    def block_diag_out(m):
        m = m.reshape(n_slab, gps, SSM_GROUP, n_state)
        return jnp.einsum('sghp,gk->sgpkh', m, eye).reshape(n_slab, half, V7X_LANES)

    bmat = jnp.concatenate([block_diag_in(b_bar.real), block_diag_in(b_bar.imag)], axis=2)
    cmat = jnp.concatenate([block_diag_out(c_re.astype(F32)), block_diag_out(-c_im.astype(F32))],
                           axis=1)
    lam_rows = jnp.stack([lam_bar.real.reshape(n_slab, half), lam_bar.imag.reshape(n_slab, half)],
                         axis=1)
    lam_rows = jnp.broadcast_to(lam_rows[:, :, None, :], (n_slab, 2, BATCH_GROUP, half))
    dvec = d_skip.astype(F32).reshape(n_slab, 1, V7X_LANES)
    return bmat.astype(BF16), cmat.astype(BF16), lam_rows, dvec


def _layer(x, c, w_ada, b_ada, g_norm_mix, w_in, ssm_lam_re, ssm_lam_im, ssm_log_dt, ssm_b_re,
           ssm_b_im, ssm_c_re, ssm_c_im, ssm_d, w_glu, b_glu, w_pool, b_pool, pool_scale, w_out,
           g_norm_ffn, w_up, w_conv, b_conv, w_down, g_final):
    bsz, seq, d = x.shape
    ssm_w = ssm_d.shape[0]
    assert bsz % BATCH_GROUP == 0 and ssm_w % V7X_LANES == 0
    tm = min(64, seq)
    tm_ffn = min(512, seq)
    assert seq % tm == 0 and seq % tm_ffn == 0 and tm >= POOL_HALO

    mod = _adaln(c, w_ada, b_ada)
    u_tm, p = _mix_in(x, mod, g_norm_mix, w_in.astype(BF16), tm=tm, ssm_w=ssm_w)
    bmat, cmat, lam_rows, dvec = _ssm_params(ssm_lam_re, ssm_lam_im, ssm_log_dt, ssm_b_re,
                                             ssm_b_im, ssm_c_re, ssm_c_im, ssm_d)
    y_tm = _s5(u_tm, bmat, cmat, lam_rows, dvec, tt=tm)
    x1 = _mix_out(x, p, y_tm, mod, w_glu.astype(BF16), b_glu, w_pool.astype(BF16), b_pool,
                  pool_scale, w_out.astype(BF16), tm=tm)
    return _ffn(x1, mod, g_norm_ffn, w_up.astype(BF16), w_conv, b_conv, w_down.astype(BF16),
                g_final, tm=tm_ffn, fc=256)


def kernel(x, c, w_ada, b_ada, g_norm_mix, w_in, ssm_lam_re, ssm_lam_im, ssm_log_dt, ssm_b_re,
           ssm_b_im, ssm_c_re, ssm_c_im, ssm_d, w_glu, b_glu, w_pool, b_pool, pool_scale, w_out,
           g_norm_ffn, w_up, w_conv, b_conv, w_down, g_norm_final):
    assert w_ada.shape[0] == 1, "single-layer block"
    return _layer(x, c, w_ada[0], b_ada[0], g_norm_mix[0], w_in[0], ssm_lam_re[0], ssm_lam_im[0],
                  ssm_log_dt[0], ssm_b_re[0], ssm_b_im[0], ssm_c_re[0], ssm_c_im[0], ssm_d[0],
                  w_glu[0], b_glu[0], w_pool[0], b_pool[0], pool_scale[0], w_out[0],
                  g_norm_ffn[0], w_up[0], w_conv[0], b_conv[0], w_down[0], g_norm_final)
```

```python
import functools

import jax
import jax.numpy as jnp
from jax import lax
from jax.experimental import pallas as pl
from jax.experimental.pallas import tpu as pltpu

V7X_LANES = 128
V7X_SUBLANES = 8
V7X_VMEM_LIMIT_BYTES = 56 * 1024 * 1024

BATCH_GROUP = V7X_SUBLANES

SSM_GROUP = 16
SSM_STATE = 64
SSM_BLOCK = 8
POOL_WINDOWS = (2, 4, 8, 16)
POOL_HALO = 16
CONV_WIDTH = 3
CONV_HALO = 8
N_MOD = 6
EPS = 1e-6

F32 = jnp.float32
BF16 = jnp.bfloat16


def _const_spec(shape):
    zeros = (0,) * len(shape)
    return pl.BlockSpec(shape, lambda *_: zeros, pipeline_mode=pl.Buffered(1))


def _rms_norm(x, g):
    ms = jnp.mean(x * x, axis=-1, keepdims=True)
    return x * lax.rsqrt(ms + EPS) * g


def _adaln_kernel(c_ref, w_ref, b_ref, o_ref):
    c = c_ref[...]
    s = c * jax.nn.sigmoid(c)
    o_ref[...] = jnp.dot(s.astype(BF16), w_ref[...].astype(BF16),
                         preferred_element_type=F32) + b_ref[...]


def _adaln(c, w_ada, b_ada):
    bsz, d = c.shape
    n = w_ada.shape[1]
    return pl.pallas_call(
        _adaln_kernel,
        grid=(n // d,),
        in_specs=[
            pl.BlockSpec((bsz, d), lambda k: (0, 0)),
            pl.BlockSpec((d, d), lambda k: (0, k)),
            pl.BlockSpec((1, d), lambda k: (0, k)),
        ],
        out_specs=pl.BlockSpec((bsz, d), lambda k: (0, k)),
        out_shape=jax.ShapeDtypeStruct((bsz, n), F32),
        name="adaln",
    )(c, w_ada, b_ada.reshape(1, n))


def _mix_in_kernel(x_ref, sh_ref, sc_ref, g_ref, w_ref, u_ref, p_ref, *, tm, ssm_w):
    nb, _, d = x_ref.shape
    h = _rms_norm(x_ref[...], g_ref[...])
    h = h * (1.0 + sc_ref[...][:, None, :]) + sh_ref[...][:, None, :]
    z = jnp.dot(h.reshape(nb * tm, d).astype(BF16), w_ref[...], preferred_element_type=F32)
    p_ref[...] = z[:, ssm_w:].reshape(p_ref.shape)
    for b in range(nb):
        for j in range(ssm_w // V7X_LANES):
            u_ref[0, j, pl.ds(b, tm, stride=nb), :] = (
                z[b * tm:(b + 1) * tm, j * V7X_LANES:(j + 1) * V7X_LANES])


def _mix_in(x, mod, g_norm, w_in_bf, *, tm, ssm_w):
    bsz, seq, d = x.shape
    mix_w = w_in_bf.shape[1]
    pool_w = mix_w - ssm_w
    nb = BATCH_GROUP
    n_slab = ssm_w // V7X_LANES
    kern = functools.partial(_mix_in_kernel, tm=tm, ssm_w=ssm_w)
    return pl.pallas_call(
        kern,
        grid=(bsz // nb, seq // tm),
        in_specs=[
            pl.BlockSpec((nb, tm, d), lambda g, i: (g, i, 0)),
            pl.BlockSpec((nb, d), lambda g, i: (g, 0)),
            pl.BlockSpec((nb, d), lambda g, i: (g, 1)),
            _const_spec((1, d)),
            _const_spec((d, mix_w)),
        ],
        out_specs=[
            pl.BlockSpec((1, n_slab, tm * nb, V7X_LANES), lambda g, i: (g, 0, i, 0)),
            pl.BlockSpec((nb, tm, pool_w), lambda g, i: (g, i, 0)),
        ],
        out_shape=[
            jax.ShapeDtypeStruct((bsz // nb, n_slab, seq * nb, V7X_LANES), F32),
            jax.ShapeDtypeStruct((bsz, seq, pool_w), F32),
        ],
        compiler_params=pltpu.CompilerParams(
            dimension_semantics=("arbitrary", "arbitrary"),
            vmem_limit_bytes=V7X_VMEM_LIMIT_BYTES),
        name="mix_in",
    )(x, mod, mod, g_norm.reshape(1, d), w_in_bf)


def _granule_transpose(v):
    granule = lax.broadcasted_iota(jnp.int32, v[0].shape, 1) // SSM_GROUP
    v = list(v)
    for d in (4, 2, 1):
        hi = (granule & d) != 0
        nxt = list(v)
        for i in range(len(v)):
            if i & d:
                continue
            lo_v, hi_v = v[i], v[i + d]
            nxt[i] = jnp.where(hi, pltpu.roll(hi_v, SSM_GROUP * d, axis=1), lo_v)
            nxt[i + d] = jnp.where(hi, hi_v, pltpu.roll(lo_v, V7X_LANES - SSM_GROUP * d, axis=1))
        v = nxt
    return v


def _s5_kernel(u_ref, toep_ref, ba_ref, ca_ref, lam_ref, d_ref, y_ref, ublk_ref, w_ref, state_ref,
               *, tt):
    n_slab = u_ref.shape[1]
    n_pair = toep_ref.shape[1]
    nb = BATCH_GROUP
    nk = tt // SSM_BLOCK
    m = nk * nb
    tile = 2 * V7X_LANES

    @pl.when(pl.program_id(1) == 0)
    def _():
        state_ref[...] = jnp.zeros_like(state_ref)

    for j in range(n_slab):
        u4 = u_ref[0, j].reshape(nk, SSM_BLOCK, nb, V7X_LANES)
        steps = [u4[:, s].reshape(m, V7X_LANES) for s in range(SSM_BLOCK)]
        ub = jnp.concatenate(_granule_transpose(steps), axis=1).astype(BF16)
        ublk_ref[j] = ub
        for pr in range(n_pair):
            w_ref[j, :, pr * tile:(pr + 1) * tile] = jnp.dot(
                ub[:, pr * tile:(pr + 1) * tile], ba_ref[j, pr], preferred_element_type=F32)

    chains = [(j, pr) for j in range(n_slab) for pr in range(n_pair)]
    half_n = len(chains) // 2
    for chunk in (chains[:half_n], chains[half_n:]):
        lam = [(lam_ref[j, pr, 0], lam_ref[j, pr, 1]) for j, pr in chunk]
        init = tuple((state_ref[j, pr, 0], state_ref[j, pr, 1]) for j, pr in chunk)

        def body(k, carry, chunk=chunk, lam=lam):
            r0 = pl.multiple_of(k * nb, nb)
            out = []
            for (j, pr), (a_re, a_im), (s_re, s_im) in zip(chunk, lam, carry):
                c_re = slice(pr * tile, pr * tile + V7X_LANES)
                c_im = slice(pr * tile + V7X_LANES, (pr + 1) * tile)
                w_re = w_ref[j, pl.ds(r0, nb), c_re]
                w_im = w_ref[j, pl.ds(r0, nb), c_im]
                w_ref[j, pl.ds(r0, nb), c_re] = s_re
                w_ref[j, pl.ds(r0, nb), c_im] = s_im
                out.append((a_re * s_re - a_im * s_im + w_re, a_re * s_im + a_im * s_re + w_im))
            return tuple(out)

        final = lax.fori_loop(0, nk, body, init)
        for (j, pr), (s_re, s_im) in zip(chunk, final):
            state_ref[j, pr, 0] = s_re
            state_ref[j, pr, 1] = s_im

    for j in range(n_slab):
        groups = []
        for pr in range(n_pair):
            cols = slice(pr * tile, (pr + 1) * tile)
            yb = jnp.dot(ublk_ref[j, :, cols], toep_ref[j, pr], preferred_element_type=F32)
            yb = yb + jnp.dot(w_ref[j, :, cols].astype(BF16), ca_ref[j, pr],
                              preferred_element_type=F32)
            groups += [yb[:, :V7X_LANES], yb[:, V7X_LANES:]]
        steps = _granule_transpose(groups)
        y4 = jnp.stack([s.reshape(nk, nb, V7X_LANES) for s in steps], axis=1)
        y_ref[0, j] = y4.reshape(tt * nb, V7X_LANES) + d_ref[j] * u_ref[0, j]


def _s5(u_tm, toep, ba, ca, lam, dvec, *, tt):
    n_bg, n_slab, rows, _ = u_tm.shape
    nb = BATCH_GROUP
    seq = rows // nb
    n_pair = toep.shape[1]
    m = tt // SSM_BLOCK * nb
    width = n_pair * 2 * V7X_LANES
    kern = functools.partial(_s5_kernel, tt=tt)
    return pl.pallas_call(
        kern,
        grid=(n_bg, seq // tt),
        in_specs=[
            pl.BlockSpec((1, n_slab, tt * nb, V7X_LANES), lambda g, i: (g, 0, i, 0)),
            _const_spec(toep.shape),
            _const_spec(ba.shape),
            _const_spec(ca.shape),
            _const_spec(lam.shape),
            _const_spec(dvec.shape),
        ],
        out_specs=pl.BlockSpec((1, n_slab, tt * nb, V7X_LANES), lambda g, i: (g, 0, i, 0)),
        out_shape=jax.ShapeDtypeStruct(u_tm.shape, F32),
        scratch_shapes=[
            pltpu.VMEM((n_slab, m, width), BF16),
            pltpu.VMEM((n_slab, m, width), F32),
            pltpu.VMEM((n_slab, n_pair, 2, nb, V7X_LANES), F32),
        ],
        compiler_params=pltpu.CompilerParams(
            dimension_semantics=("arbitrary", "arbitrary"),
            vmem_limit_bytes=V7X_VMEM_LIMIT_BYTES),
        name="s5_mixer",
    )(u_tm, toep, ba, ca, lam, dvec)


def _gelu_tanh(x):
    return 0.5 * x * (1.0 + jnp.tanh(0.7978845608028654 * (x + 0.044715 * (x * x * x))))


def _mix_out_kernel(x_ref, p_ref, y_ref, gt_ref, wglu_ref, bglu_ref, wpool_ref, bpool_ref,
                    pscale_ref, wout_ref, o_ref, ynat_ref, pbuf_ref, *, tm):
    nb, _, d = x_ref.shape
    ssm_w = ynat_ref.shape[1]
    pool_w = p_ref.shape[2]
    pgw = pool_w // len(POOL_WINDOWS)
    i = pl.program_id(1)

    for b in range(nb):
        for j in range(ssm_w // V7X_LANES):
            ynat_ref[b * tm:(b + 1) * tm, j * V7X_LANES:(j + 1) * V7X_LANES] = (
                y_ref[0, j, pl.ds(b, tm, stride=nb), :])
    yg = _gelu_tanh(ynat_ref[...]).astype(BF16)
    t = jnp.dot(yg, wglu_ref[...], preferred_element_type=F32) + bglu_ref[...]
    y_ssm = t[:, :ssm_w] * jax.nn.sigmoid(t[:, ssm_w:])

    @pl.when(i == 0)
    def _():
        pbuf_ref[:, 0:POOL_HALO, :] = jnp.zeros((nb, POOL_HALO, pool_w), F32)

    @pl.when(i > 0)
    def _():
        pbuf_ref[:, 0:POOL_HALO, :] = pbuf_ref[:, tm:tm + POOL_HALO, :]

    pbuf_ref[:, POOL_HALO:, :] = p_ref[...]
    pos = (i * tm + 1 + lax.broadcasted_iota(jnp.int32, (1, tm, 1), 1)).astype(F32)
    y_pool = []
    for gi, w in enumerate(POOL_WINDOWS):
        cols = slice(gi * pgw, (gi + 1) * pgw)
        tok = pbuf_ref[:, POOL_HALO:POOL_HALO + tm, cols]
        acc = tok
        for k in range(1, w):
            acc = acc + pbuf_ref[:, POOL_HALO - k:POOL_HALO - k + tm, cols]
        z = acc / jnp.minimum(pos, float(w)) - tok
        zp = jnp.dot(z.reshape(nb * tm, pgw).astype(BF16), wpool_ref[gi],
                     preferred_element_type=F32)
        y_pool.append((zp + bpool_ref[:, cols]) * pscale_ref[:, cols])
    y_pool = jnp.concatenate(y_pool, axis=-1)

    mixed = jnp.dot(y_ssm.astype(BF16), wout_ref[:ssm_w, :], preferred_element_type=F32)
    mixed = mixed + jnp.dot(y_pool.astype(BF16), wout_ref[ssm_w:, :], preferred_element_type=F32)
    o_ref[...] = x_ref[...] + gt_ref[...][:, None, :] * mixed.reshape(nb, tm, d)


def _mix_out(x, p, y_tm, mod, w_glu_bf, b_glu, w_pool_bf, b_pool, pool_scale, w_out_bf, *, tm):
    bsz, seq, d = x.shape
    pool_w = p.shape[2]
    n_bg, n_slab, _, _ = y_tm.shape
    ssm_w = n_slab * V7X_LANES
    nb = BATCH_GROUP
    kern = functools.partial(_mix_out_kernel, tm=tm)
    return pl.pallas_call(
        kern,
        grid=(n_bg, seq // tm),
        in_specs=[
            pl.BlockSpec((nb, tm, d), lambda g, i: (g, i, 0)),
            pl.BlockSpec((nb, tm, pool_w), lambda g, i: (g, i, 0)),
            pl.BlockSpec((1, n_slab, tm * nb, V7X_LANES), lambda g, i: (g, 0, i, 0)),
            pl.BlockSpec((nb, d), lambda g, i: (g, 2)),
            _const_spec(w_glu_bf.shape),
            _const_spec((1, 2 * ssm_w)),
            _const_spec(w_pool_bf.shape),
            _const_spec((1, pool_w)),
            _const_spec((1, pool_w)),
            _const_spec(w_out_bf.shape),
        ],
        out_specs=pl.BlockSpec((nb, tm, d), lambda g, i: (g, i, 0)),
        out_shape=jax.ShapeDtypeStruct(x.shape, F32),
        scratch_shapes=[
            pltpu.VMEM((nb * tm, ssm_w), F32),
            pltpu.VMEM((nb, POOL_HALO + tm, pool_w), F32),
        ],
        compiler_params=pltpu.CompilerParams(
            dimension_semantics=("arbitrary", "arbitrary"),
            vmem_limit_bytes=V7X_VMEM_LIMIT_BYTES),
        name="mix_out",
    )(x, p, y_tm, mod, w_glu_bf, b_glu.reshape(1, -1), w_pool_bf, b_pool.reshape(1, -1),
      pool_scale.reshape(1, -1), w_out_bf)


def _ffn_kernel(x_ref, sh_ref, sc_ref, gt_ref, gn_ref, wup_ref, wconv_ref, bconv_ref, wdown_ref,
                gfin_ref, o_ref, gbuf_ref, gprev_ref, acc_ref, *, tm, fc):
    d = x_ref.shape[2]
    d_ff = wdown_ref.shape[0]
    x = x_ref[0]
    h = _rms_norm(x, gn_ref[...])
    h = (h * (1.0 + sc_ref[0, 0]) + sh_ref[0, 0]).astype(BF16)

    @pl.when(pl.program_id(1) == 0)
    def _():
        gprev_ref[...] = jnp.zeros_like(gprev_ref)

    for c in range(d_ff // fc):
        cols = slice(c * fc, (c + 1) * fc)
        v = jnp.dot(h, wup_ref[:, c * fc:(c + 1) * fc], preferred_element_type=F32)
        g = jnp.dot(h, wup_ref[:, d_ff + c * fc:d_ff + (c + 1) * fc], preferred_element_type=F32)
        gbuf_ref[0:CONV_HALO, :] = gprev_ref[:, cols]
        gbuf_ref[CONV_HALO:, :] = g
        gprev_ref[:, cols] = g[tm - CONV_HALO:, :]
        gc = bconv_ref[:, cols] + wconv_ref[CONV_WIDTH - 1:CONV_WIDTH, cols] * g
        for k in range(1, CONV_WIDTH):
            gc = gc + (wconv_ref[CONV_WIDTH - 1 - k:CONV_WIDTH - k, cols]
                       * gbuf_ref[CONV_HALO - k:CONV_HALO - k + tm, :])
        a = (gc * jax.nn.sigmoid(gc) * v).astype(BF16)
        part = jnp.dot(a, wdown_ref[cols, :], preferred_element_type=F32)
        if c == 0:
            acc_ref[...] = part
        else:
            acc_ref[...] += part

    y = x + gt_ref[0, 0] * acc_ref[...]
    o_ref[0] = _rms_norm(y, gfin_ref[...])


def _ffn(x, mod, g_norm, w_up_bf, w_conv, b_conv, w_down_bf, g_final, *, tm, fc):
    bsz, seq, d = x.shape
    d_ff = w_down_bf.shape[0]
    mod4 = mod.reshape(bsz, N_MOD, 1, d)
    kern = functools.partial(_ffn_kernel, tm=tm, fc=fc)
    return pl.pallas_call(
        kern,
        grid=(bsz, seq // tm),
        in_specs=[
            pl.BlockSpec((1, tm, d), lambda b, i: (b, i, 0)),
            pl.BlockSpec((1, 1, 1, d), lambda b, i: (b, 3, 0, 0)),
            pl.BlockSpec((1, 1, 1, d), lambda b, i: (b, 4, 0, 0)),
            pl.BlockSpec((1, 1, 1, d), lambda b, i: (b, 5, 0, 0)),
            _const_spec((1, d)),
            _const_spec(w_up_bf.shape),
            _const_spec(w_conv.shape),
            _const_spec((1, d_ff)),
            _const_spec(w_down_bf.shape),
            _const_spec((1, d)),
        ],
        out_specs=pl.BlockSpec((1, tm, d), lambda b, i: (b, i, 0)),
        out_shape=jax.ShapeDtypeStruct(x.shape, F32),
        scratch_shapes=[
            pltpu.VMEM((CONV_HALO + tm, fc), F32),
            pltpu.VMEM((CONV_HALO, d_ff), F32),
            pltpu.VMEM((tm, d), F32),
        ],
        compiler_params=pltpu.CompilerParams(
            dimension_semantics=("arbitrary", "arbitrary"),
            vmem_limit_bytes=V7X_VMEM_LIMIT_BYTES),
        name="ffn",
    )(x, mod4, mod4, mod4, g_norm.reshape(1, d), w_up_bf, w_conv, b_conv.reshape(1, d_ff),
      w_down_bf, g_final.reshape(1, d))


def _ssm_params(lam_re, lam_im, log_dt, b_re, b_im, c_re, c_im, d_skip):
    n_groups, n_state = lam_re.shape
    h, t = SSM_GROUP, SSM_BLOCK
    gps = V7X_LANES // h
    n_slab, n_pair = n_groups // gps, gps // 2
    lam_re, lam_im = lam_re.astype(F32), lam_im.astype(F32)
    dt = jnp.exp(log_dt.astype(F32))[:, None]

    k = jnp.arange(t + 1, dtype=F32)[:, None, None]
    mag = jnp.exp(k * (dt * lam_re))
    p_re, p_im = mag * jnp.cos(k * (dt * lam_im)), mag * jnp.sin(k * (dt * lam_im))

    a_re, a_im = p_re[1], p_im[1]
    den = lam_re * lam_re + lam_im * lam_im
    q_re = ((a_re - 1.0) * lam_re + a_im * lam_im) / den
    q_im = (a_im * lam_re - (a_re - 1.0) * lam_im) / den
    b_re, b_im = b_re.astype(F32), b_im.astype(F32)
    bb_re = q_re[..., None] * b_re - q_im[..., None] * b_im
    bb_im = q_re[..., None] * b_im + q_im[..., None] * b_re
    c_re, c_im = c_re.astype(F32)[None], c_im.astype(F32)[None]

    cak_re = c_re * p_re[:, :, None, :] - c_im * p_im[:, :, None, :]
    cak_im = c_re * p_im[:, :, None, :] + c_im * p_re[:, :, None, :]
    kern = (jnp.einsum('kgop,gpi->kgoi', cak_re[:t], bb_re)
            - jnp.einsum('kgop,gpi->kgoi', cak_im[:t], bb_im))
    lag = jnp.arange(t)[None, :] - jnp.arange(t)[:, None]
    toep = jnp.where((lag >= 0)[:, :, None, None, None], kern[jnp.maximum(lag, 0)], 0.0)
    toep = toep.transpose(2, 0, 4, 1, 3).reshape(n_groups, t * h, t * h)
    r_re, r_im = p_re[t - 1::-1][..., None], p_im[t - 1::-1][..., None]
    ba = jnp.stack([r_re * bb_re - r_im * bb_im, r_re * bb_im + r_im * bb_re], axis=0)
    ba = ba.transpose(2, 1, 4, 0, 3).reshape(n_groups, t * h, 2, n_state)
    ca = jnp.stack([cak_re[1:], -cak_im[1:]], axis=0)
    ca = ca.transpose(2, 0, 4, 1, 3).reshape(n_groups, 2, n_state, t * h)

    eye2 = jnp.eye(2, dtype=F32)
    toep = toep.reshape(n_slab, n_pair, 2, t * h, t * h)
    toep = jnp.einsum('jpgab,gk->jpgakb', toep, eye2).reshape(n_slab, n_pair, 2 * t * h, 2 * t * h)
    ba = ba.reshape(n_slab, n_pair, 2, t * h, 2, n_state)
    ba = jnp.einsum('jpgacn,gk->jpgackn', ba, eye2).reshape(n_slab, n_pair, 2 * t * h, 4 * n_state)
    ca = ca.reshape(n_slab, n_pair, 2, 2, n_state, t * h)
    ca = jnp.einsum('jpgcnb,gk->jpcgnkb', ca, eye2).reshape(n_slab, n_pair, 4 * n_state, 2 * t * h)

    lam_t = jnp.stack([p_re[t].reshape(n_slab, n_pair, 2 * n_state),
                       p_im[t].reshape(n_slab, n_pair, 2 * n_state)], axis=2)
    lam_t = jnp.broadcast_to(lam_t[:, :, :, None, :], (n_slab, n_pair, 2, BATCH_GROUP, 2 * n_state))
    dvec = d_skip.astype(F32).reshape(n_slab, 1, V7X_LANES)
    return toep.astype(BF16), ba.astype(BF16), ca.astype(BF16), lam_t, dvec


def _layer(x, c, w_ada, b_ada, g_norm_mix, w_in, ssm_lam_re, ssm_lam_im, ssm_log_dt, ssm_b_re,
           ssm_b_im, ssm_c_re, ssm_c_im, ssm_d, w_glu, b_glu, w_pool, b_pool, pool_scale, w_out,
           g_norm_ffn, w_up, w_conv, b_conv, w_down, g_final):
    bsz, seq, d = x.shape
    ssm_w = ssm_d.shape[0]
    assert bsz % BATCH_GROUP == 0 and ssm_w % V7X_LANES == 0
    tm = min(64, seq)
    tm_ffn = min(512, seq)
    assert seq % tm == 0 and seq % tm_ffn == 0 and tm >= POOL_HALO

    mod = _adaln(c, w_ada, b_ada)
    u_tm, p = _mix_in(x, mod, g_norm_mix, w_in.astype(BF16), tm=tm, ssm_w=ssm_w)
    toep, ba, ca, lam_t, dvec = _ssm_params(ssm_lam_re, ssm_lam_im, ssm_log_dt, ssm_b_re,
                                            ssm_b_im, ssm_c_re, ssm_c_im, ssm_d)
    y_tm = _s5(u_tm, toep, ba, ca, lam_t, dvec, tt=min(256, seq))
    x1 = _mix_out(x, p, y_tm, mod, w_glu.astype(BF16), b_glu, w_pool.astype(BF16), b_pool,
                  pool_scale, w_out.astype(BF16), tm=tm)
    return _ffn(x1, mod, g_norm_ffn, w_up.astype(BF16), w_conv, b_conv, w_down.astype(BF16),
                g_final, tm=tm_ffn, fc=256)


def kernel(x, c, w_ada, b_ada, g_norm_mix, w_in, ssm_lam_re, ssm_lam_im, ssm_log_dt, ssm_b_re,
           ssm_b_im, ssm_c_re, ssm_c_im, ssm_d, w_glu, b_glu, w_pool, b_pool, pool_scale, w_out,
           g_norm_ffn, w_up, w_conv, b_conv, w_down, g_norm_final):
    assert w_ada.shape[0] == 1, "single-layer block"
    return _layer(x, c, w_ada[0], b_ada[0], g_norm_mix[0], w_in[0], ssm_lam_re[0], ssm_lam_im[0],
                  ssm_log_dt[0], ssm_b_re[0], ssm_b_im[0], ssm_c_re[0], ssm_c_im[0], ssm_d[0],
                  w_glu[0], b_glu[0], w_pool[0], b_pool[0], pool_scale[0], w_out[0],
                  g_norm_ffn[0], w_up[0], w_conv[0], b_conv[0], w_down[0], g_norm_final)
```

```python
import functools

import jax
import jax.numpy as jnp
from jax import lax
from jax.experimental import pallas as pl
from jax.experimental.pallas import tpu as pltpu

V7X_LANES = 128
V7X_SUBLANES = 8
V7X_VMEM_LIMIT_BYTES = 56 * 1024 * 1024

BATCH_GROUP = V7X_SUBLANES

SSM_GROUP = 16
SSM_STATE = 64
SSM_BLOCK = 8
POOL_WINDOWS = (2, 4, 8, 16)
POOL_HALO = 16
POOL_PAD = 8
CONV_WIDTH = 3
CONV_HALO = 8
N_MOD = 6
EPS = 1e-6

F32 = jnp.float32
BF16 = jnp.bfloat16


def _tile_sizes(seq):
    return min(128, seq), min(512, seq), 256


def _const_spec(shape):
    zeros = (0,) * len(shape)
    return pl.BlockSpec(shape, lambda *_: zeros, pipeline_mode=pl.Buffered(1))


def _rms_norm(x, g):
    ms = jnp.mean(x * x, axis=-1, keepdims=True)
    return x * lax.rsqrt(ms + EPS) * g


def _zero_row(width, *arrays):
    acc = None
    for a in arrays:
        bits = lax.bitcast_convert_type(a.astype(F32), jnp.int32)
        rows, cols = bits.shape
        blocks = [bits[r:r + V7X_SUBLANES] for r in range(0, rows, V7X_SUBLANES)]
        bits = functools.reduce(jnp.bitwise_or, blocks)
        tiles = [bits[:, c:c + V7X_LANES] for c in range(0, cols, V7X_LANES)]
        bits = functools.reduce(jnp.bitwise_or, tiles)
        acc = bits if acc is None else acc | bits
    zero = lax.shift_right_logical(lax.shift_right_logical(acc, 16), 16).astype(F32)
    return jnp.concatenate([zero[0:1]] * (width // V7X_LANES), axis=1)


def _adaln_kernel(c_ref, w_ref, b_ref, o_ref):
    c = c_ref[...]
    s = c * jax.nn.sigmoid(c)
    o_ref[...] = jnp.dot(s.astype(BF16), w_ref[...].astype(BF16),
                         preferred_element_type=F32) + b_ref[...]


def _adaln(c, w_ada, b_ada):
    bsz, d = c.shape
    n = w_ada.shape[1]
    return pl.pallas_call(
        _adaln_kernel,
        grid=(n // d,),
        in_specs=[
            pl.BlockSpec((bsz, d), lambda k: (0, 0)),
            pl.BlockSpec((d, d), lambda k: (0, k)),
            pl.BlockSpec((1, d), lambda k: (0, k)),
        ],
        out_specs=pl.BlockSpec((bsz, d), lambda k: (0, k)),
        out_shape=jax.ShapeDtypeStruct((bsz, n), F32),
        name="adaln",
    )(c, w_ada, b_ada.reshape(1, n))


def _gelu_tanh(x, one=1.0):
    return 0.5 * x * (one + jnp.tanh(0.7978845608028654 * (x + 0.044715 * (x * x * x))))


def _granule_transpose(v):
    granule = lax.broadcasted_iota(jnp.int32, v[0].shape, 1) // SSM_GROUP
    v = list(v)
    for d in (4, 2, 1):
        hi = (granule & d) != 0
        nxt = list(v)
        for i in range(len(v)):
            if i & d:
                continue
            lo_v, hi_v = v[i], v[i + d]
            nxt[i] = jnp.where(hi, pltpu.roll(hi_v, SSM_GROUP * d, axis=1), lo_v)
            nxt[i + d] = jnp.where(hi, hi_v, pltpu.roll(lo_v, V7X_LANES - SSM_GROUP * d, axis=1))
        v = nxt
    return v


def _mixer_kernel(x_ref, sh_ref, sc_ref, gt_ref, gmix_ref, win_ref, toep_ref, ba_ref, ca_ref, lam_ref,
                  wglu_ref, bglu_ref, wpool_ref, bpool_ref, pscale_ref, wout_ref, o_ref,
                  utm_ref, ytm_ref, yblk_ref, state_ref, halo_ref, lvl_ref, ublk0, ublk1, w0, w1, pbuf0,
                  pbuf1,
                  *, tt, n_tiles, n_total):
    s = pl.program_id(0)
    first_front = jnp.minimum(s, n_total - 1) % n_tiles == 0
    i_back = jnp.maximum(s - 1, 0) % n_tiles
    first_back = i_back == 0
    nb, _, d = x_ref.shape
    n_slab, n_pair = toep_ref.shape[0], toep_ref.shape[1]
    ssm_w = n_slab * V7X_LANES
    pool_w = halo_ref.shape[2]
    pgw = pool_w // len(POOL_WINDOWS)
    nk = tt // SSM_BLOCK
    m = nk * nb
    tile = 2 * V7X_LANES

    @pl.when(s == 0)
    def _():
        ublk1[...] = jnp.zeros_like(ublk1)
        w1[...] = jnp.zeros_like(w1)
        pbuf1[...] = jnp.zeros_like(pbuf1)
        pbuf0[:, 0:POOL_PAD, :] = jnp.zeros((nb, POOL_PAD, pool_w), F32)
        lvl_ref[...] = jnp.zeros_like(lvl_ref)
        halo_ref[...] = jnp.zeros_like(halo_ref)
        state_ref[...] = jnp.zeros_like(state_ref)

    def step(ublk_f, w_f, pbuf_f, ublk_b, w_b, pbuf_b):
        h = _rms_norm(x_ref[...], gmix_ref[...])
        h = h * (1.0 + sc_ref[...][:, None, :]) + sh_ref[...][:, None, :]

        for j in range(n_slab):
            for pr in range(n_pair):
                cols = slice(pr * tile, (pr + 1) * tile)
                yb = jnp.dot(ublk_b[j, :, cols], toep_ref[j, pr], preferred_element_type=F32)
                yblk_ref[j, :, cols] = yb + jnp.dot(w_b[j, :, cols].astype(BF16), ca_ref[j, pr],
                                                    preferred_element_type=F32)

        z = jnp.dot(h.reshape(nb * tt, d).astype(BF16), win_ref[...], preferred_element_type=F32)
        pbuf_f[:, POOL_PAD + POOL_HALO:, :] = z[:, ssm_w:].reshape(nb, tt, pool_w)
        for b in range(nb):
            for j in range(n_slab):
                utm_ref[j, pl.ds(b, tt, stride=nb), :] = (
                    z[b * tt:(b + 1) * tt, j * V7X_LANES:(j + 1) * V7X_LANES])

        for j in range(n_slab):
            groups = [yblk_ref[j, :, g * V7X_LANES:(g + 1) * V7X_LANES] for g in range(2 * n_pair)]
            steps = _granule_transpose(groups)
            y4 = jnp.stack([v.reshape(nk, nb, V7X_LANES) for v in steps], axis=1)
            ytm_ref[j] = y4.reshape(tt * nb, V7X_LANES)
        y_nat = jnp.concatenate(
            [jnp.concatenate([ytm_ref[j, pl.ds(b, tt, stride=nb), :] for j in range(n_slab)], axis=1)
             for b in range(nb)], axis=0)
        one = 1.0 + _zero_row(ssm_w, z[:, :V7X_LANES], z[:, -V7X_LANES:])
        yg = _gelu_tanh(y_nat, one).astype(BF16)
        t = jnp.dot(yg, wglu_ref[...], preferred_element_type=F32) + bglu_ref[...]
        y_ssm = (t[:, :ssm_w] * jax.nn.sigmoid(t[:, ssm_w:])).astype(BF16)

        w_taps = []
        for j in range(n_slab):
            u4 = utm_ref[j].reshape(nk, SSM_BLOCK, nb, V7X_LANES)
            steps = [u4[:, q].reshape(m, V7X_LANES) for q in range(SSM_BLOCK)]
            ub = jnp.concatenate(_granule_transpose(steps), axis=1).astype(BF16)
            ublk_f[j] = ub
            for pr in range(n_pair):
                wp = jnp.dot(ub[:, pr * tile:(pr + 1) * tile], ba_ref[j, pr],
                             preferred_element_type=F32)
                w_f[j, :, pr * tile:(pr + 1) * tile] = wp
                w_taps.append(wp[:, -V7X_LANES:])
        bpool = bpool_ref[...] + _zero_row(pool_w, *w_taps)

        lo, hi = POOL_PAD, POOL_PAD + POOL_HALO + tt
        pbuf_b[:, lo:lo + POOL_HALO, :] = jnp.where(first_back, 0.0, halo_ref[...])
        halo_ref[...] = pbuf_b[:, hi - POOL_HALO:hi, :]
        pos = (i_back * tt + 1 + lax.broadcasted_iota(jnp.int32, (1, tt, 1), 1)).astype(F32)
        y_pool = []
        for gi, win in enumerate(POOL_WINDOWS):
            cols = slice(gi * pgw, (gi + 1) * pgw)
            tok = pbuf_b[:, lo + POOL_HALO:hi, cols]
            sums = pbuf_b[:, lo:hi, cols] + pbuf_b[:, lo - 1:hi - 1, cols]
            span, level = 2, 0
            while span < win:
                lv = lvl_ref.at[level % 2]
                lv[:, lo:hi, :] = sums
                sums = sums + lv[:, lo - span:hi - span, :]
                span, level = 2 * span, level + 1
            zc = sums[:, POOL_HALO:, :] * (1.0 / jnp.minimum(pos, float(win))) - tok
            zp = jnp.dot(zc.reshape(nb * tt, pgw).astype(BF16), wpool_ref[gi],
                         preferred_element_type=F32)
            y_pool.append((zp + bpool[:, cols]) * pscale_ref[:, cols])
        y_pool = jnp.concatenate(y_pool, axis=-1).astype(BF16)

        chains = [(j, pr) for j in range(n_slab) for pr in range(n_pair)]
        state = [(jnp.where(first_front, 0.0, state_ref[j, pr, 0]),
                  jnp.where(first_front, 0.0, state_ref[j, pr, 1])) for j, pr in chains]
        for k in range(nk):
            rows = slice(k * nb, (k + 1) * nb)
            for ci, (j, pr) in enumerate(chains):
                c_re = slice(pr * tile, pr * tile + V7X_LANES)
                c_im = slice(pr * tile + V7X_LANES, (pr + 1) * tile)
                a_re, a_im = lam_ref[j, pr, 0], lam_ref[j, pr, 1]
                s_re, s_im = state[ci]
                w_re, w_im = w_f[j, rows, c_re], w_f[j, rows, c_im]
                w_f[j, rows, c_re] = s_re
                w_f[j, rows, c_im] = s_im
                state[ci] = (a_re * s_re - a_im * s_im + w_re, a_re * s_im + a_im * s_re + w_im)
        for (j, pr), (s_re, s_im) in zip(chains, state):
            state_ref[j, pr, 0] = s_re
            state_ref[j, pr, 1] = s_im

        mixed = jnp.dot(y_ssm, wout_ref[:ssm_w, :], preferred_element_type=F32)
        mixed = mixed + jnp.dot(y_pool, wout_ref[ssm_w:, :], preferred_element_type=F32)
        o_ref[...] = gt_ref[...][:, None, :] * mixed.reshape(nb, tt, d)

    @pl.when(s % 2 == 0)
    def _():
        step(ublk0, w0, pbuf0, ublk1, w1, pbuf1)

    @pl.when(s % 2 == 1)
    def _():
        step(ublk1, w1, pbuf1, ublk0, w0, pbuf0)


def _mixer(x, mod, g_norm, w_in_bf, toep, ba, ca, lam_t, w_glu_bf, b_glu, w_pool_bf, b_pool, pool_scale,
           w_out_bf, *, tt):
    bsz, seq, d = x.shape
    nb = BATCH_GROUP
    n_slab, n_pair = toep.shape[0], toep.shape[1]
    ssm_w = n_slab * V7X_LANES
    pool_w = w_in_bf.shape[1] - ssm_w
    n_tiles = seq // tt
    n_total = (bsz // nb) * n_tiles
    m = tt // SSM_BLOCK * nb
    width = n_pair * 2 * V7X_LANES

    def front(s):
        t = jnp.minimum(s, n_total - 1)
        return t // n_tiles, t % n_tiles

    def back(s):
        t = jnp.maximum(s - 1, 0)
        return t // n_tiles, t % n_tiles

    kern = functools.partial(_mixer_kernel, tt=tt, n_tiles=n_tiles, n_total=n_total)
    return pl.pallas_call(
        kern,
        grid=(n_total + 1,),
        in_specs=[
            pl.BlockSpec((nb, tt, d), lambda s: (*front(s), 0)),
            pl.BlockSpec((nb, d), lambda s: (front(s)[0], 0)),
            pl.BlockSpec((nb, d), lambda s: (front(s)[0], 1)),
            pl.BlockSpec((nb, d), lambda s: (back(s)[0], 2)),
            _const_spec((1, d)),
            _const_spec(w_in_bf.shape),
            _const_spec(toep.shape),
            _const_spec(ba.shape),
            _const_spec(ca.shape),
            _const_spec(lam_t.shape),
            _const_spec(w_glu_bf.shape),
            _const_spec((1, 2 * ssm_w)),
            _const_spec(w_pool_bf.shape),
            _const_spec((1, pool_w)),
            _const_spec((1, pool_w)),
            _const_spec(w_out_bf.shape),
        ],
        out_specs=pl.BlockSpec((nb, tt, d), lambda s: (*back(s), 0)),
        out_shape=jax.ShapeDtypeStruct(x.shape, F32),
        scratch_shapes=[
            pltpu.VMEM((n_slab, tt * nb, V7X_LANES), F32),
            pltpu.VMEM((n_slab, tt * nb, V7X_LANES), F32),
            pltpu.VMEM((n_slab, m, width), F32),
            pltpu.VMEM((n_slab, n_pair, 2, nb, V7X_LANES), F32),
            pltpu.VMEM((nb, POOL_HALO, pool_w), F32),
            pltpu.VMEM((2, nb, POOL_PAD + POOL_HALO + tt, pool_w // len(POOL_WINDOWS)), F32),
            pltpu.VMEM((n_slab, m, width), BF16),
            pltpu.VMEM((n_slab, m, width), BF16),
            pltpu.VMEM((n_slab, m, width), F32),
            pltpu.VMEM((n_slab, m, width), F32),
            pltpu.VMEM((nb, POOL_PAD + POOL_HALO + tt, pool_w), F32),
            pltpu.VMEM((nb, POOL_PAD + POOL_HALO + tt, pool_w), F32),
        ],
        compiler_params=pltpu.CompilerParams(
            dimension_semantics=("arbitrary",),
            vmem_limit_bytes=V7X_VMEM_LIMIT_BYTES),
        name="mixer",
    )(x, mod, mod, mod, g_norm.reshape(1, d), w_in_bf, toep, ba, ca, lam_t, w_glu_bf,
      b_glu.reshape(1, -1), w_pool_bf, b_pool.reshape(1, -1), pool_scale.reshape(1, -1), w_out_bf)


FFN_ROW_SLICES = 8


def _ffn_kernel(xf_ref, dlf_ref, sh_ref, sc_ref, xb_ref, dlb_ref, gt_ref, gn_ref, wup_ref, wconv_ref,
                bconv_ref, wdown_ref, gfin_ref, o_ref, gbuf_ref, gprev_ref, h0, h1, acc0, acc1,
                *, tm, fc, n_tiles):
    s = pl.program_id(0)
    first_mid = jnp.maximum(s - 1, 0) % n_tiles == 0
    d_ff = wdown_ref.shape[0]
    n_chunks = d_ff // fc
    rs = tm // FFN_ROW_SLICES

    @pl.when(s == 0)
    def _():
        h1[...] = jnp.zeros_like(h1)
        acc0[...] = jnp.zeros_like(acc0)
        gprev_ref[...] = jnp.zeros_like(gprev_ref)

    def step(h_f, h_m, acc_m, acc_b):
        def up(c):
            cols = slice(c * fc, (c + 1) * fc)
            gb = gbuf_ref.at[c % 2]
            h = h_m[...]
            v = jnp.dot(h, wup_ref[:, c * fc:(c + 1) * fc], preferred_element_type=F32)
            g = jnp.dot(h, wup_ref[:, d_ff + c * fc:d_ff + (c + 1) * fc],
                        preferred_element_type=F32)
            gb[0:CONV_HALO, :] = jnp.where(first_mid, 0.0, gprev_ref[:, cols])
            gb[CONV_HALO:, :] = g
            gprev_ref[:, cols] = g[tm - CONV_HALO:, :]
            return v

        v = up(0)
        anchor = None
        for c in range(n_chunks):
            cols = slice(c * fc, (c + 1) * fc)
            v_next = up(c + 1) if c + 1 < n_chunks else None
            bias = bconv_ref[:, cols]
            if anchor is not None:
                bias = bias + anchor
            gb = gbuf_ref.at[c % 2]
            gc = bias + wconv_ref[CONV_WIDTH - 1:CONV_WIDTH, cols] * gb[CONV_HALO:, :]
            for k in range(1, CONV_WIDTH):
                gc = gc + (wconv_ref[CONV_WIDTH - 1 - k:CONV_WIDTH - k, cols]
                           * gb[CONV_HALO - k:CONV_HALO - k + tm, :])
            a = (gc * jax.nn.sigmoid(gc) * v).astype(BF16)
            part = jnp.dot(a, wdown_ref[cols, :], preferred_element_type=F32)
            if c == 0:
                acc_m[...] = part
            else:
                acc_m[...] += part
            v = v_next
            anchor = None
            if c < FFN_ROW_SLICES:
                rows = slice(c * rs, (c + 1) * rs)
                y = xb_ref[0, rows, :] + dlb_ref[0, rows, :] + gt_ref[0, 0] * acc_b[rows, :]
                y = _rms_norm(y, gfin_ref[...])
                o_ref[0, rows, :] = y
                hf = _rms_norm(xf_ref[0, rows, :] + dlf_ref[0, rows, :], gn_ref[...])
                hf = hf * (1.0 + sc_ref[0, 0]) + sh_ref[0, 0]
                h_f[rows, :] = hf.astype(BF16)
                anchor = _zero_row(fc, y, hf)

    @pl.when(s % 2 == 0)
    def _():
        step(h0, h1, acc1, acc0)

    @pl.when(s % 2 == 1)
    def _():
        step(h1, h0, acc0, acc1)


def _ffn(x, delta, mod, g_norm, w_up_bf, w_conv, b_conv, w_down_bf, g_final, *, tm, fc):
    bsz, seq, d = x.shape
    d_ff = w_down_bf.shape[0]
    n_tiles = seq // tm
    n_total = bsz * n_tiles
    assert d_ff % fc == 0 and d_ff // fc >= FFN_ROW_SLICES and tm % (FFN_ROW_SLICES * V7X_SUBLANES) == 0
    mod4 = mod.reshape(bsz, N_MOD, 1, d)

    def front(s):
        t = jnp.minimum(s, n_total - 1)
        return t // n_tiles, t % n_tiles

    def back(s):
        t = jnp.maximum(s - 2, 0)
        return t // n_tiles, t % n_tiles

    kern = functools.partial(_ffn_kernel, tm=tm, fc=fc, n_tiles=n_tiles)
    return pl.pallas_call(
        kern,
        grid=(n_total + 2,),
        in_specs=[
            pl.BlockSpec((1, tm, d), lambda s: (*front(s), 0)),
            pl.BlockSpec((1, tm, d), lambda s: (*front(s), 0)),
            pl.BlockSpec((1, 1, 1, d), lambda s: (front(s)[0], 3, 0, 0)),
            pl.BlockSpec((1, 1, 1, d), lambda s: (front(s)[0], 4, 0, 0)),
            pl.BlockSpec((1, tm, d), lambda s: (*back(s), 0)),
            pl.BlockSpec((1, tm, d), lambda s: (*back(s), 0)),
            pl.BlockSpec((1, 1, 1, d), lambda s: (back(s)[0], 5, 0, 0)),
            _const_spec((1, d)),
            _const_spec(w_up_bf.shape),
            _const_spec(w_conv.shape),
            _const_spec((1, d_ff)),
            _const_spec(w_down_bf.shape),
            _const_spec((1, d)),
        ],
        out_specs=pl.BlockSpec((1, tm, d), lambda s: (*back(s), 0)),
        out_shape=jax.ShapeDtypeStruct(x.shape, F32),
        scratch_shapes=[
            pltpu.VMEM((2, CONV_HALO + tm, fc), F32),
            pltpu.VMEM((CONV_HALO, d_ff), F32),
            pltpu.VMEM((tm, d), BF16),
            pltpu.VMEM((tm, d), BF16),
            pltpu.VMEM((tm, d), F32),
            pltpu.VMEM((tm, d), F32),
        ],
        compiler_params=pltpu.CompilerParams(
            dimension_semantics=("arbitrary",),
            vmem_limit_bytes=V7X_VMEM_LIMIT_BYTES),
        name="ffn",
    )(x, delta, mod4, mod4, x, delta, mod4, g_norm.reshape(1, d), w_up_bf, w_conv,
      b_conv.reshape(1, d_ff), w_down_bf, g_final.reshape(1, d))


def _ssm_params(lam_re, lam_im, log_dt, b_re, b_im, c_re, c_im, d_skip):
    n_groups, n_state = lam_re.shape
    h, t = SSM_GROUP, SSM_BLOCK
    gps = V7X_LANES // h
    n_slab, n_pair = n_groups // gps, gps // 2
    lam_re, lam_im = lam_re.astype(F32), lam_im.astype(F32)
    dt = jnp.exp(log_dt.astype(F32))[:, None]

    k = jnp.arange(t + 1, dtype=F32)[:, None, None]
    mag = jnp.exp(k * (dt * lam_re))
    p_re, p_im = mag * jnp.cos(k * (dt * lam_im)), mag * jnp.sin(k * (dt * lam_im))

    a_re, a_im = p_re[1], p_im[1]
    den = lam_re * lam_re + lam_im * lam_im
    q_re = ((a_re - 1.0) * lam_re + a_im * lam_im) / den
    q_im = (a_im * lam_re - (a_re - 1.0) * lam_im) / den
    b_re, b_im = b_re.astype(F32), b_im.astype(F32)
    bb_re = q_re[..., None] * b_re - q_im[..., None] * b_im
    bb_im = q_re[..., None] * b_im + q_im[..., None] * b_re
    c_re, c_im = c_re.astype(F32)[None], c_im.astype(F32)[None]

    cak_re = c_re * p_re[:, :, None, :] - c_im * p_im[:, :, None, :]
    cak_im = c_re * p_im[:, :, None, :] + c_im * p_re[:, :, None, :]
    kern = (jnp.einsum('kgop,gpi->kgoi', cak_re[:t], bb_re)
            - jnp.einsum('kgop,gpi->kgoi', cak_im[:t], bb_im))
    d_diag = d_skip.astype(F32).reshape(n_groups, h)[:, :, None] * jnp.eye(h, dtype=F32)
    kern = kern.at[0].add(d_diag)
    lag = jnp.arange(t)[None, :] - jnp.arange(t)[:, None]
    toep = jnp.where((lag >= 0)[:, :, None, None, None], kern[jnp.maximum(lag, 0)], 0.0)
    toep = toep.transpose(2, 0, 4, 1, 3).reshape(n_groups, t * h, t * h)
    r_re, r_im = p_re[t - 1::-1][..., None], p_im[t - 1::-1][..., None]
    ba = jnp.stack([r_re * bb_re - r_im * bb_im, r_re * bb_im + r_im * bb_re], axis=0)
    ba = ba.transpose(2, 1, 4, 0, 3).reshape(n_groups, t * h, 2, n_state)
    ca = jnp.stack([cak_re[1:], -cak_im[1:]], axis=0)
    ca = ca.transpose(2, 0, 4, 1, 3).reshape(n_groups, 2, n_state, t * h)

    eye2 = jnp.eye(2, dtype=F32)
    toep = toep.reshape(n_slab, n_pair, 2, t * h, t * h)
    toep = jnp.einsum('jpgab,gk->jpgakb', toep, eye2).reshape(n_slab, n_pair, 2 * t * h, 2 * t * h)
    ba = ba.reshape(n_slab, n_pair, 2, t * h, 2, n_state)
    ba = jnp.einsum('jpgacn,gk->jpgackn', ba, eye2).reshape(n_slab, n_pair, 2 * t * h, 4 * n_state)
    ca = ca.reshape(n_slab, n_pair, 2, 2, n_state, t * h)
    ca = jnp.einsum('jpgcnb,gk->jpcgnkb', ca, eye2).reshape(n_slab, n_pair, 4 * n_state, 2 * t * h)

    lam_t = jnp.stack([p_re[t].reshape(n_slab, n_pair, 2 * n_state),
                       p_im[t].reshape(n_slab, n_pair, 2 * n_state)], axis=2)
    lam_t = jnp.broadcast_to(lam_t[:, :, :, None, :], (n_slab, n_pair, 2, BATCH_GROUP, 2 * n_state))
    return toep.astype(BF16), ba.astype(BF16), ca.astype(BF16), lam_t


def _layer(x, c, w_ada, b_ada, g_norm_mix, w_in, ssm_lam_re, ssm_lam_im, ssm_log_dt, ssm_b_re,
           ssm_b_im, ssm_c_re, ssm_c_im, ssm_d, w_glu, b_glu, w_pool, b_pool, pool_scale, w_out,
           g_norm_ffn, w_up, w_conv, b_conv, w_down, g_final):
    bsz, seq, d = x.shape
    ssm_w = ssm_d.shape[0]
    tt, tm_ffn, fc = _tile_sizes(seq)
    assert bsz % BATCH_GROUP == 0 and ssm_w % V7X_LANES == 0
    assert seq % tt == 0 and seq % tm_ffn == 0 and tt >= POOL_HALO and tt % SSM_BLOCK == 0

    mod = _adaln(c, w_ada, b_ada)
    toep, ba, ca, lam_t = _ssm_params(ssm_lam_re, ssm_lam_im, ssm_log_dt, ssm_b_re, ssm_b_im,
                                      ssm_c_re, ssm_c_im, ssm_d)
    delta = _mixer(x, mod, g_norm_mix, w_in.astype(BF16), toep, ba, ca, lam_t, w_glu.astype(BF16),
                   b_glu, w_pool.astype(BF16), b_pool, pool_scale, w_out.astype(BF16), tt=tt)
    return _ffn(x, delta, mod, g_norm_ffn, w_up.astype(BF16), w_conv, b_conv, w_down.astype(BF16),
                g_final, tm=tm_ffn, fc=fc)


def kernel(x, c, w_ada, b_ada, g_norm_mix, w_in, ssm_lam_re, ssm_lam_im, ssm_log_dt, ssm_b_re,
           ssm_b_im, ssm_c_re, ssm_c_im, ssm_d, w_glu, b_glu, w_pool, b_pool, pool_scale, w_out,
           g_norm_ffn, w_up, w_conv, b_conv, w_down, g_norm_final):
    assert w_ada.shape[0] == 1, "single-layer block"
    return _layer(x, c, w_ada[0], b_ada[0], g_norm_mix[0], w_in[0], ssm_lam_re[0], ssm_lam_im[0],
                  ssm_log_dt[0], ssm_b_re[0], ssm_b_im[0], ssm_c_re[0], ssm_c_im[0], ssm_d[0],
                  w_glu[0], b_glu[0], w_pool[0], b_pool[0], pool_scale[0], w_out[0],
                  g_norm_ffn[0], w_up[0], w_conv[0], b_conv[0], w_down[0], g_norm_final)
```

```python
import functools

import jax
import jax.numpy as jnp
from jax import lax
from jax.experimental import pallas as pl
from jax.experimental.pallas import tpu as pltpu

V7X_LANES = 128
V7X_SUBLANES = 8
V7X_VMEM_LIMIT_BYTES = 56 * 1024 * 1024

BATCH_GROUP = V7X_SUBLANES

SSM_GROUP = 16
SSM_STATE = 64
SSM_BLOCK = 8
POOL_WINDOWS = (2, 4, 8, 16)
POOL_HALO = 16
POOL_PAD = 8
CONV_WIDTH = 3
CONV_HALO = 8
N_MOD = 6
EPS = 1e-6

F32 = jnp.float32
BF16 = jnp.bfloat16


def _tile_sizes(seq):
    return min(128, seq), min(512, seq), 256


def _const_spec(shape):
    zeros = (0,) * len(shape)
    return pl.BlockSpec(shape, lambda *_: zeros, pipeline_mode=pl.Buffered(1))


def _rms_norm(x, g):
    ms = jnp.mean(x * x, axis=-1, keepdims=True)
    return x * lax.rsqrt(ms + EPS) * g


def _zero_row(width, *arrays):
    acc = None
    for a in arrays:
        bits = lax.bitcast_convert_type(a.astype(F32), jnp.int32)
        rows, cols = bits.shape
        blocks = [bits[r:r + V7X_SUBLANES] for r in range(0, rows, V7X_SUBLANES)]
        bits = functools.reduce(jnp.bitwise_or, blocks)
        tiles = [bits[:, c:c + V7X_LANES] for c in range(0, cols, V7X_LANES)]
        bits = functools.reduce(jnp.bitwise_or, tiles)
        acc = bits if acc is None else acc | bits
    zero = lax.shift_right_logical(lax.shift_right_logical(acc, 16), 16).astype(F32)
    return jnp.concatenate([zero[0:1]] * (width // V7X_LANES), axis=1)


def _adaln_kernel(c_ref, w_ref, b_ref, o_ref):
    c = c_ref[...]
    s = c * jax.nn.sigmoid(c)
    o_ref[...] = jnp.dot(s.astype(BF16), w_ref[...].astype(BF16),
                         preferred_element_type=F32) + b_ref[...]


def _adaln(c, w_ada, b_ada):
    bsz, d = c.shape
    n = w_ada.shape[1]
    return pl.pallas_call(
        _adaln_kernel,
        grid=(n // d,),
        in_specs=[
            pl.BlockSpec((bsz, d), lambda k: (0, 0)),
            pl.BlockSpec((d, d), lambda k: (0, k)),
            pl.BlockSpec((1, d), lambda k: (0, k)),
        ],
        out_specs=pl.BlockSpec((bsz, d), lambda k: (0, k)),
        out_shape=jax.ShapeDtypeStruct((bsz, n), F32),
        name="adaln",
    )(c, w_ada, b_ada.reshape(1, n))


def _gelu_tanh(x, one=1.0):
    return 0.5 * x * (one + jnp.tanh(0.7978845608028654 * (x + 0.044715 * (x * x * x))))


def _granule_transpose(v):
    granule = lax.broadcasted_iota(jnp.int32, v[0].shape, 1) // SSM_GROUP
    v = list(v)
    for d in (4, 2, 1):
        hi = (granule & d) != 0
        nxt = list(v)
        for i in range(len(v)):
            if i & d:
                continue
            lo_v, hi_v = v[i], v[i + d]
            nxt[i] = jnp.where(hi, pltpu.roll(hi_v, SSM_GROUP * d, axis=1), lo_v)
            nxt[i + d] = jnp.where(hi, hi_v, pltpu.roll(lo_v, V7X_LANES - SSM_GROUP * d, axis=1))
        v = nxt
    return v


def _mixer_kernel(x_ref, sh_ref, sc_ref, gt_ref, gmix_ref, win_ref, toep_ref, ba_ref, ca_ref, lam_ref,
                  wglu_ref, bglu_ref, wpool_ref, bpool_ref, pscale_ref, wout_ref, o_ref,
                  utm_ref, ytm_ref, yblk_ref, state_ref, halo_ref, lvl_ref, ublk0, ublk1, w0, w1, pbuf0,
                  pbuf1,
                  *, tt, n_tiles, n_total):
    s = pl.program_id(0)
    first_front = jnp.minimum(s, n_total - 1) % n_tiles == 0
    i_back = jnp.maximum(s - 1, 0) % n_tiles
    first_back = i_back == 0
    nb, _, d = x_ref.shape
    n_slab, n_pair = toep_ref.shape[0], toep_ref.shape[1]
    ssm_w = n_slab * V7X_LANES
    pool_w = halo_ref.shape[2]
    pgw = pool_w // len(POOL_WINDOWS)
    nk = tt // SSM_BLOCK
    m = nk * nb
    tile = 2 * V7X_LANES

    @pl.when(s == 0)
    def _():
        ublk1[...] = jnp.zeros_like(ublk1)
        w1[...] = jnp.zeros_like(w1)
        pbuf1[...] = jnp.zeros_like(pbuf1)
        pbuf0[:, 0:POOL_PAD, :] = jnp.zeros((nb, POOL_PAD, pool_w), F32)
        lvl_ref[...] = jnp.zeros_like(lvl_ref)
        halo_ref[...] = jnp.zeros_like(halo_ref)
        state_ref[...] = jnp.zeros_like(state_ref)

    def step(ublk_f, w_f, pbuf_f, ublk_b, w_b, pbuf_b):
        h = _rms_norm(x_ref[...], gmix_ref[...])
        h = h * (1.0 + sc_ref[...][:, None, :]) + sh_ref[...][:, None, :]

        for j in range(n_slab):
            for pr in range(n_pair):
                cols = slice(pr * tile, (pr + 1) * tile)
                yb = jnp.dot(ublk_b[j, :, cols], toep_ref[j, pr], preferred_element_type=F32)
                yblk_ref[j, :, cols] = yb + jnp.dot(w_b[j, :, cols].astype(BF16), ca_ref[j, pr],
                                                    preferred_element_type=F32)

        z = jnp.dot(h.reshape(nb * tt, d).astype(BF16), win_ref[...], preferred_element_type=F32)
        pbuf_f[:, POOL_PAD + POOL_HALO:, :] = z[:, ssm_w:].reshape(nb, tt, pool_w)
        for b in range(nb):
            for j in range(n_slab):
                utm_ref[j, pl.ds(b, tt, stride=nb), :] = (
                    z[b * tt:(b + 1) * tt, j * V7X_LANES:(j + 1) * V7X_LANES])

        for j in range(n_slab):
            groups = [yblk_ref[j, :, g * V7X_LANES:(g + 1) * V7X_LANES] for g in range(2 * n_pair)]
            steps = _granule_transpose(groups)
            y4 = jnp.stack([v.reshape(nk, nb, V7X_LANES) for v in steps], axis=1)
            ytm_ref[j] = y4.reshape(tt * nb, V7X_LANES)
        y_nat = jnp.concatenate(
            [jnp.concatenate([ytm_ref[j, pl.ds(b, tt, stride=nb), :] for j in range(n_slab)], axis=1)
             for b in range(nb)], axis=0)
        one = 1.0 + _zero_row(ssm_w, z[:, :V7X_LANES], z[:, -V7X_LANES:])
        yg = _gelu_tanh(y_nat, one).astype(BF16)
        t = jnp.dot(yg, wglu_ref[...], preferred_element_type=F32) + bglu_ref[...]
        y_ssm = (t[:, :ssm_w] * jax.nn.sigmoid(t[:, ssm_w:])).astype(BF16)
        o_ref[...] = jnp.dot(y_ssm, wout_ref[:ssm_w, :],
                             preferred_element_type=F32).reshape(nb, tt, d)

        w_taps = []
        for j in range(n_slab):
            u4 = utm_ref[j].reshape(nk, SSM_BLOCK, nb, V7X_LANES)
            steps = [u4[:, q].reshape(m, V7X_LANES) for q in range(SSM_BLOCK)]
            ub = jnp.concatenate(_granule_transpose(steps), axis=1).astype(BF16)
            ublk_f[j] = ub
            for pr in range(n_pair):
                wp = jnp.dot(ub[:, pr * tile:(pr + 1) * tile], ba_ref[j, pr],
                             preferred_element_type=F32)
                w_f[j, :, pr * tile:(pr + 1) * tile] = wp
                w_taps.append(wp[:, -V7X_LANES:])
        bpool = bpool_ref[...] + _zero_row(pool_w, *w_taps)

        lo, hi = POOL_PAD, POOL_PAD + POOL_HALO + tt
        pbuf_b[:, lo:lo + POOL_HALO, :] = jnp.where(first_back, 0.0, halo_ref[...])
        halo_ref[...] = pbuf_b[:, hi - POOL_HALO:hi, :]
        pos = (i_back * tt + 1 + lax.broadcasted_iota(jnp.int32, (1, tt, 1), 1)).astype(F32)
        y_pool = []
        for gi, win in enumerate(POOL_WINDOWS):
            cols = slice(gi * pgw, (gi + 1) * pgw)
            tok = pbuf_b[:, lo + POOL_HALO:hi, cols]
            sums = pbuf_b[:, lo:hi, cols] + pbuf_b[:, lo - 1:hi - 1, cols]
            span, level = 2, 0
            while span < win:
                lv = lvl_ref.at[level % 2]
                lv[:, lo:hi, :] = sums
                sums = sums + lv[:, lo - span:hi - span, :]
                span, level = 2 * span, level + 1
            zc = sums[:, POOL_HALO:, :] * (1.0 / jnp.minimum(pos, float(win))) - tok
            zp = jnp.dot(zc.reshape(nb * tt, pgw).astype(BF16), wpool_ref[gi],
                         preferred_element_type=F32)
            y_pool.append((zp + bpool[:, cols]) * pscale_ref[:, cols])
        y_pool = jnp.concatenate(y_pool, axis=-1).astype(BF16)

        chains = [(j, pr) for j in range(n_slab) for pr in range(n_pair)]
        state = [(jnp.where(first_front, 0.0, state_ref[j, pr, 0]),
                  jnp.where(first_front, 0.0, state_ref[j, pr, 1])) for j, pr in chains]
        for k in range(nk):
            rows = slice(k * nb, (k + 1) * nb)
            for ci, (j, pr) in enumerate(chains):
                c_re = slice(pr * tile, pr * tile + V7X_LANES)
                c_im = slice(pr * tile + V7X_LANES, (pr + 1) * tile)
                a_re, a_im = lam_ref[j, pr, 0], lam_ref[j, pr, 1]
                s_re, s_im = state[ci]
                w_re, w_im = w_f[j, rows, c_re], w_f[j, rows, c_im]
                w_f[j, rows, c_re] = s_re
                w_f[j, rows, c_im] = s_im
                state[ci] = (a_re * s_re - a_im * s_im + w_re, a_re * s_im + a_im * s_re + w_im)
        for (j, pr), (s_re, s_im) in zip(chains, state):
            state_ref[j, pr, 0] = s_re
            state_ref[j, pr, 1] = s_im

        mixed = jnp.dot(y_pool, wout_ref[ssm_w:, :], preferred_element_type=F32)
        o_ref[...] = gt_ref[...][:, None, :] * (o_ref[...] + mixed.reshape(nb, tt, d))

    @pl.when(s % 2 == 0)
    def _():
        step(ublk0, w0, pbuf0, ublk1, w1, pbuf1)

    @pl.when(s % 2 == 1)
    def _():
        step(ublk1, w1, pbuf1, ublk0, w0, pbuf0)


def _mixer(x, mod, g_norm, w_in_bf, toep, ba, ca, lam_t, w_glu_bf, b_glu, w_pool_bf, b_pool, pool_scale,
           w_out_bf, *, tt):
    bsz, seq, d = x.shape
    nb = BATCH_GROUP
    n_slab, n_pair = toep.shape[0], toep.shape[1]
    ssm_w = n_slab * V7X_LANES
    pool_w = w_in_bf.shape[1] - ssm_w
    n_tiles = seq // tt
    n_total = (bsz // nb) * n_tiles
    m = tt // SSM_BLOCK * nb
    width = n_pair * 2 * V7X_LANES

    def front(s):
        t = jnp.minimum(s, n_total - 1)
        return t // n_tiles, t % n_tiles

    def back(s):
        t = jnp.maximum(s - 1, 0)
        return t // n_tiles, t % n_tiles

    kern = functools.partial(_mixer_kernel, tt=tt, n_tiles=n_tiles, n_total=n_total)
    return pl.pallas_call(
        kern,
        grid=(n_total + 1,),
        in_specs=[
            pl.BlockSpec((nb, tt, d), lambda s: (*front(s), 0)),
            pl.BlockSpec((nb, d), lambda s: (front(s)[0], 0)),
            pl.BlockSpec((nb, d), lambda s: (front(s)[0], 1)),
            pl.BlockSpec((nb, d), lambda s: (back(s)[0], 2)),
            _const_spec((1, d)),
            _const_spec(w_in_bf.shape),
            _const_spec(toep.shape),
            _const_spec(ba.shape),
            _const_spec(ca.shape),
            _const_spec(lam_t.shape),
            _const_spec(w_glu_bf.shape),
            _const_spec((1, 2 * ssm_w)),
            _const_spec(w_pool_bf.shape),
            _const_spec((1, pool_w)),
            _const_spec((1, pool_w)),
            _const_spec(w_out_bf.shape),
        ],
        out_specs=pl.BlockSpec((nb, tt, d), lambda s: (*back(s), 0)),
        out_shape=jax.ShapeDtypeStruct(x.shape, F32),
        scratch_shapes=[
            pltpu.VMEM((n_slab, tt * nb, V7X_LANES), F32),
            pltpu.VMEM((n_slab, tt * nb, V7X_LANES), F32),
            pltpu.VMEM((n_slab, m, width), F32),
            pltpu.VMEM((n_slab, n_pair, 2, nb, V7X_LANES), F32),
            pltpu.VMEM((nb, POOL_HALO, pool_w), F32),
            pltpu.VMEM((2, nb, POOL_PAD + POOL_HALO + tt, pool_w // len(POOL_WINDOWS)), F32),
            pltpu.VMEM((n_slab, m, width), BF16),
            pltpu.VMEM((n_slab, m, width), BF16),
            pltpu.VMEM((n_slab, m, width), F32),
            pltpu.VMEM((n_slab, m, width), F32),
            pltpu.VMEM((nb, POOL_PAD + POOL_HALO + tt, pool_w), F32),
            pltpu.VMEM((nb, POOL_PAD + POOL_HALO + tt, pool_w), F32),
        ],
        compiler_params=pltpu.CompilerParams(
            dimension_semantics=("arbitrary",),
            vmem_limit_bytes=V7X_VMEM_LIMIT_BYTES),
        name="mixer",
    )(x, mod, mod, mod, g_norm.reshape(1, d), w_in_bf, toep, ba, ca, lam_t, w_glu_bf,
      b_glu.reshape(1, -1), w_pool_bf, b_pool.reshape(1, -1), pool_scale.reshape(1, -1), w_out_bf)


FFN_ROW_SLICES = 8
FFN_UP_AHEAD = 2


def _ffn_kernel(xf_ref, dlf_ref, sh_ref, sc_ref, xb_ref, dlb_ref, gt_ref, gn_ref, wup_ref, wconv_ref,
                bconv_ref, wdown_ref, gfin_ref, o_ref, gbuf_ref, gprev_ref, h0, h1, acc0, acc1,
                *, tm, fc, n_tiles):
    s = pl.program_id(0)
    first_mid = jnp.maximum(s - 1, 0) % n_tiles == 0
    d_ff = wdown_ref.shape[0]
    n_chunks = d_ff // fc
    n_gbuf = gbuf_ref.shape[0]
    rs = tm // FFN_ROW_SLICES

    @pl.when(s == 0)
    def _():
        h1[...] = jnp.zeros_like(h1)
        acc0[...] = jnp.zeros_like(acc0)
        gprev_ref[...] = jnp.zeros_like(gprev_ref)

    def step(h_f, h_m, acc_m, acc_b):
        def up(c):
            cols = slice(c * fc, (c + 1) * fc)
            gb = gbuf_ref.at[c % n_gbuf]
            h = h_m[...]
            v = jnp.dot(h, wup_ref[:, c * fc:(c + 1) * fc], preferred_element_type=F32)
            g = jnp.dot(h, wup_ref[:, d_ff + c * fc:d_ff + (c + 1) * fc],
                        preferred_element_type=F32)
            gb[0:CONV_HALO, :] = jnp.where(first_mid, 0.0, gprev_ref[:, cols])
            gb[CONV_HALO:, :] = g
            gprev_ref[:, cols] = g[tm - CONV_HALO:, :]
            return v

        pending = {c: up(c) for c in range(FFN_UP_AHEAD)}
        anchor = None
        for c in range(n_chunks):
            cols = slice(c * fc, (c + 1) * fc)
            if c + FFN_UP_AHEAD < n_chunks:
                pending[c + FFN_UP_AHEAD] = up(c + FFN_UP_AHEAD)
            v = pending.pop(c)
            bias = bconv_ref[:, cols]
            if anchor is not None:
                bias = bias + anchor
            gb = gbuf_ref.at[c % n_gbuf]
            gc = bias + wconv_ref[CONV_WIDTH - 1:CONV_WIDTH, cols] * gb[CONV_HALO:, :]
            for k in range(1, CONV_WIDTH):
                gc = gc + (wconv_ref[CONV_WIDTH - 1 - k:CONV_WIDTH - k, cols]
                           * gb[CONV_HALO - k:CONV_HALO - k + tm, :])
            a = (gc * jax.nn.sigmoid(gc) * v).astype(BF16)
            part = jnp.dot(a, wdown_ref[cols, :], preferred_element_type=F32)
            if c == 0:
                acc_m[...] = part
            else:
                acc_m[...] += part
            anchor = None
            if c < FFN_ROW_SLICES:
                rows = slice(c * rs, (c + 1) * rs)
                y = xb_ref[0, rows, :] + dlb_ref[0, rows, :] + gt_ref[0, 0] * acc_b[rows, :]
                y = _rms_norm(y, gfin_ref[...])
                o_ref[0, rows, :] = y
                hf = _rms_norm(xf_ref[0, rows, :] + dlf_ref[0, rows, :], gn_ref[...])
                hf = hf * (1.0 + sc_ref[0, 0]) + sh_ref[0, 0]
                h_f[rows, :] = hf.astype(BF16)
                anchor = _zero_row(fc, y, hf)

    @pl.when(s % 2 == 0)
    def _():
        step(h0, h1, acc1, acc0)

    @pl.when(s % 2 == 1)
    def _():
        step(h1, h0, acc0, acc1)


def _ffn(x, delta, mod, g_norm, w_up_bf, w_conv, b_conv, w_down_bf, g_final, *, tm, fc):
    bsz, seq, d = x.shape
    d_ff = w_down_bf.shape[0]
    n_tiles = seq // tm
    n_total = bsz * n_tiles
    assert d_ff % fc == 0 and d_ff // fc >= FFN_ROW_SLICES and tm % (FFN_ROW_SLICES * V7X_SUBLANES) == 0
    mod4 = mod.reshape(bsz, N_MOD, 1, d)

    def front(s):
        t = jnp.minimum(s, n_total - 1)
        return t // n_tiles, t % n_tiles

    def back(s):
        t = jnp.maximum(s - 2, 0)
        return t // n_tiles, t % n_tiles

    kern = functools.partial(_ffn_kernel, tm=tm, fc=fc, n_tiles=n_tiles)
    return pl.pallas_call(
        kern,
        grid=(n_total + 2,),
        in_specs=[
            pl.BlockSpec((1, tm, d), lambda s: (*front(s), 0)),
            pl.BlockSpec((1, tm, d), lambda s: (*front(s), 0)),
            pl.BlockSpec((1, 1, 1, d), lambda s: (front(s)[0], 3, 0, 0)),
            pl.BlockSpec((1, 1, 1, d), lambda s: (front(s)[0], 4, 0, 0)),
            pl.BlockSpec((1, tm, d), lambda s: (*back(s), 0)),
            pl.BlockSpec((1, tm, d), lambda s: (*back(s), 0)),
            pl.BlockSpec((1, 1, 1, d), lambda s: (back(s)[0], 5, 0, 0)),
            _const_spec((1, d)),
            _const_spec(w_up_bf.shape),
            _const_spec(w_conv.shape),
            _const_spec((1, d_ff)),
            _const_spec(w_down_bf.shape),
            _const_spec((1, d)),
        ],
        out_specs=pl.BlockSpec((1, tm, d), lambda s: (*back(s), 0)),
        out_shape=jax.ShapeDtypeStruct(x.shape, F32),
        scratch_shapes=[
            pltpu.VMEM((FFN_UP_AHEAD + 1, CONV_HALO + tm, fc), F32),
            pltpu.VMEM((CONV_HALO, d_ff), F32),
            pltpu.VMEM((tm, d), BF16),
            pltpu.VMEM((tm, d), BF16),
            pltpu.VMEM((tm, d), F32),
            pltpu.VMEM((tm, d), F32),
        ],
        compiler_params=pltpu.CompilerParams(
            dimension_semantics=("arbitrary",),
            vmem_limit_bytes=V7X_VMEM_LIMIT_BYTES),
        name="ffn",
    )(x, delta, mod4, mod4, x, delta, mod4, g_norm.reshape(1, d), w_up_bf, w_conv,
      b_conv.reshape(1, d_ff), w_down_bf, g_final.reshape(1, d))


def _ssm_params(lam_re, lam_im, log_dt, b_re, b_im, c_re, c_im, d_skip):
    n_groups, n_state = lam_re.shape
    h, t = SSM_GROUP, SSM_BLOCK
    gps = V7X_LANES // h
    n_slab, n_pair = n_groups // gps, gps // 2
    lam_re, lam_im = lam_re.astype(F32), lam_im.astype(F32)
    dt = jnp.exp(log_dt.astype(F32))[:, None]

    k = jnp.arange(t + 1, dtype=F32)[:, None, None]
    mag = jnp.exp(k * (dt * lam_re))
    p_re, p_im = mag * jnp.cos(k * (dt * lam_im)), mag * jnp.sin(k * (dt * lam_im))

    a_re, a_im = p_re[1], p_im[1]
    den = lam_re * lam_re + lam_im * lam_im
    q_re = ((a_re - 1.0) * lam_re + a_im * lam_im) / den
    q_im = (a_im * lam_re - (a_re - 1.0) * lam_im) / den
    b_re, b_im = b_re.astype(F32), b_im.astype(F32)
    bb_re = q_re[..., None] * b_re - q_im[..., None] * b_im
    bb_im = q_re[..., None] * b_im + q_im[..., None] * b_re
    c_re, c_im = c_re.astype(F32)[None], c_im.astype(F32)[None]

    cak_re = c_re * p_re[:, :, None, :] - c_im * p_im[:, :, None, :]
    cak_im = c_re * p_im[:, :, None, :] + c_im * p_re[:, :, None, :]
    kern = (jnp.einsum('kgop,gpi->kgoi', cak_re[:t], bb_re)
            - jnp.einsum('kgop,gpi->kgoi', cak_im[:t], bb_im))
    d_diag = d_skip.astype(F32).reshape(n_groups, h)[:, :, None] * jnp.eye(h, dtype=F32)
    kern = kern.at[0].add(d_diag)
    lag = jnp.arange(t)[None, :] - jnp.arange(t)[:, None]
    toep = jnp.where((lag >= 0)[:, :, None, None, None], kern[jnp.maximum(lag, 0)], 0.0)
    toep = toep.transpose(2, 0, 4, 1, 3).reshape(n_groups, t * h, t * h)
    r_re, r_im = p_re[t - 1::-1][..., None], p_im[t - 1::-1][..., None]
    ba = jnp.stack([r_re * bb_re - r_im * bb_im, r_re * bb_im + r_im * bb_re], axis=0)
    ba = ba.transpose(2, 1, 4, 0, 3).reshape(n_groups, t * h, 2, n_state)
    ca = jnp.stack([cak_re[1:], -cak_im[1:]], axis=0)
    ca = ca.transpose(2, 0, 4, 1, 3).reshape(n_groups, 2, n_state, t * h)

    eye2 = jnp.eye(2, dtype=F32)
    toep = toep.reshape(n_slab, n_pair, 2, t * h, t * h)
    toep = jnp.einsum('jpgab,gk->jpgakb', toep, eye2).reshape(n_slab, n_pair, 2 * t * h, 2 * t * h)
    ba = ba.reshape(n_slab, n_pair, 2, t * h, 2, n_state)
    ba = jnp.einsum('jpgacn,gk->jpgackn', ba, eye2).reshape(n_slab, n_pair, 2 * t * h, 4 * n_state)
    ca = ca.reshape(n_slab, n_pair, 2, 2, n_state, t * h)
    ca = jnp.einsum('jpgcnb,gk->jpcgnkb', ca, eye2).reshape(n_slab, n_pair, 4 * n_state, 2 * t * h)

    lam_t = jnp.stack([p_re[t].reshape(n_slab, n_pair, 2 * n_state),
                       p_im[t].reshape(n_slab, n_pair, 2 * n_state)], axis=2)
    lam_t = jnp.broadcast_to(lam_t[:, :, :, None, :], (n_slab, n_pair, 2, BATCH_GROUP, 2 * n_state))
    return toep.astype(BF16), ba.astype(BF16), ca.astype(BF16), lam_t


def _layer(x, c, w_ada, b_ada, g_norm_mix, w_in, ssm_lam_re, ssm_lam_im, ssm_log_dt, ssm_b_re,
           ssm_b_im, ssm_c_re, ssm_c_im, ssm_d, w_glu, b_glu, w_pool, b_pool, pool_scale, w_out,
           g_norm_ffn, w_up, w_conv, b_conv, w_down, g_final):
    bsz, seq, d = x.shape
    ssm_w = ssm_d.shape[0]
    tt, tm_ffn, fc = _tile_sizes(seq)
    assert bsz % BATCH_GROUP == 0 and ssm_w % V7X_LANES == 0
    assert seq % tt == 0 and seq % tm_ffn == 0 and tt >= POOL_HALO and tt % SSM_BLOCK == 0

    mod = _adaln(c, w_ada, b_ada)
    toep, ba, ca, lam_t = _ssm_params(ssm_lam_re, ssm_lam_im, ssm_log_dt, ssm_b_re, ssm_b_im,
                                      ssm_c_re, ssm_c_im, ssm_d)
    delta = _mixer(x, mod, g_norm_mix, w_in.astype(BF16), toep, ba, ca, lam_t, w_glu.astype(BF16),
                   b_glu, w_pool.astype(BF16), b_pool, pool_scale, w_out.astype(BF16), tt=tt)
    return _ffn(x, delta, mod, g_norm_ffn, w_up.astype(BF16), w_conv, b_conv, w_down.astype(BF16),
                g_final, tm=tm_ffn, fc=fc)


def kernel(x, c, w_ada, b_ada, g_norm_mix, w_in, ssm_lam_re, ssm_lam_im, ssm_log_dt, ssm_b_re,
           ssm_b_im, ssm_c_re, ssm_c_im, ssm_d, w_glu, b_glu, w_pool, b_pool, pool_scale, w_out,
           g_norm_ffn, w_up, w_conv, b_conv, w_down, g_norm_final):
    assert w_ada.shape[0] == 1, "single-layer block"
    return _layer(x, c, w_ada[0], b_ada[0], g_norm_mix[0], w_in[0], ssm_lam_re[0], ssm_lam_im[0],
                  ssm_log_dt[0], ssm_b_re[0], ssm_b_im[0], ssm_c_re[0], ssm_c_im[0], ssm_d[0],
                  w_glu[0], b_glu[0], w_pool[0], b_pool[0], pool_scale[0], w_out[0],
                  g_norm_ffn[0], w_up[0], w_conv[0], b_conv[0], w_down[0], g_norm_final)
```

```python
import functools

import jax
import jax.numpy as jnp
from jax import lax
from jax.experimental import pallas as pl
from jax.experimental.pallas import tpu as pltpu

V7X_LANES = 128
V7X_SUBLANES = 8
V7X_VMEM_LIMIT_BYTES = 56 * 1024 * 1024

BATCH_GROUP = V7X_SUBLANES

SSM_GROUP = 16
SSM_STATE = 64
SSM_BLOCK = 8
POOL_WINDOWS = (2, 4, 8, 16)
POOL_HALO = 16
POOL_PAD = 8
CONV_WIDTH = 3
CONV_HALO = 8
N_MOD = 6
EPS = 1e-6

F32 = jnp.float32
BF16 = jnp.bfloat16


def _tile_sizes(seq):
    return min(128, seq), min(512, seq), 256


def _const_spec(shape):
    zeros = (0,) * len(shape)
    return pl.BlockSpec(shape, lambda *_: zeros, pipeline_mode=pl.Buffered(1))


def _rms_norm(x, g):
    ms = jnp.mean(x * x, axis=-1, keepdims=True)
    return x * lax.rsqrt(ms + EPS) * g


def _zero_row(width, *arrays):
    acc = None
    for a in arrays:
        bits = lax.bitcast_convert_type(a.astype(F32), jnp.int32)
        rows, cols = bits.shape
        blocks = [bits[r:r + V7X_SUBLANES] for r in range(0, rows, V7X_SUBLANES)]
        bits = functools.reduce(jnp.bitwise_or, blocks)
        tiles = [bits[:, c:c + V7X_LANES] for c in range(0, cols, V7X_LANES)]
        bits = functools.reduce(jnp.bitwise_or, tiles)
        acc = bits if acc is None else acc | bits
    zero = lax.shift_right_logical(lax.shift_right_logical(acc, 16), 16).astype(F32)
    return jnp.concatenate([zero[0:1]] * (width // V7X_LANES), axis=1)


def _adaln_kernel(c_ref, w_ref, b_ref, o_ref):
    c = c_ref[...]
    s = c * jax.nn.sigmoid(c)
    o_ref[...] = jnp.dot(s.astype(BF16), w_ref[...].astype(BF16),
                         preferred_element_type=F32) + b_ref[...]


def _adaln(c, w_ada, b_ada):
    bsz, d = c.shape
    n = w_ada.shape[1]
    return pl.pallas_call(
        _adaln_kernel,
        grid=(n // d,),
        in_specs=[
            pl.BlockSpec((bsz, d), lambda k: (0, 0)),
            pl.BlockSpec((d, d), lambda k: (0, k)),
            pl.BlockSpec((1, d), lambda k: (0, k)),
        ],
        out_specs=pl.BlockSpec((bsz, d), lambda k: (0, k)),
        out_shape=jax.ShapeDtypeStruct((bsz, n), F32),
        name="adaln",
    )(c, w_ada, b_ada.reshape(1, n))


def _gelu_tanh(x, one=1.0):
    return 0.5 * x * (one + jnp.tanh(0.7978845608028654 * (x + 0.044715 * (x * x * x))))


def _granule_transpose(v):
    granule = lax.broadcasted_iota(jnp.int32, v[0].shape, 1) // SSM_GROUP
    v = list(v)
    for d in (4, 2, 1):
        hi = (granule & d) != 0
        nxt = list(v)
        for i in range(len(v)):
            if i & d:
                continue
            lo_v, hi_v = v[i], v[i + d]
            nxt[i] = jnp.where(hi, pltpu.roll(hi_v, SSM_GROUP * d, axis=1), lo_v)
            nxt[i + d] = jnp.where(hi, hi_v, pltpu.roll(lo_v, V7X_LANES - SSM_GROUP * d, axis=1))
        v = nxt
    return v


def _mixer_kernel(x_ref, sh_ref, sc_ref, gt_ref, gmix_ref, win_ref, toep_ref, ba_ref, ca_ref, lam_ref,
                  wglu_ref, bglu_ref, wpool_ref, bpool_ref, pscale_ref, wout_ref, o_ref,
                  utm_ref, ytm_ref, yblk_ref, state_ref, halo_ref, lvl_ref, ublk0, ublk1, w0, w1, pbuf0,
                  pbuf1,
                  *, tt, n_tiles, n_total):
    s = pl.program_id(0)
    first_front = jnp.minimum(s, n_total - 1) % n_tiles == 0
    i_back = jnp.maximum(s - 1, 0) % n_tiles
    first_back = i_back == 0
    nb, _, d = x_ref.shape
    n_slab, n_pair = toep_ref.shape[0], toep_ref.shape[1]
    ssm_w = n_slab * V7X_LANES
    pool_w = halo_ref.shape[2]
    pgw = pool_w // len(POOL_WINDOWS)
    nk = tt // SSM_BLOCK
    m = nk * nb
    tile = 2 * V7X_LANES

    @pl.when(s == 0)
    def _():
        ublk1[...] = jnp.zeros_like(ublk1)
        w1[...] = jnp.zeros_like(w1)
        pbuf1[...] = jnp.zeros_like(pbuf1)
        pbuf0[:, 0:POOL_PAD, :] = jnp.zeros((nb, POOL_PAD, pool_w), F32)
        lvl_ref[...] = jnp.zeros_like(lvl_ref)
        halo_ref[...] = jnp.zeros_like(halo_ref)
        state_ref[...] = jnp.zeros_like(state_ref)

    def step(ublk_f, w_f, pbuf_f, ublk_b, w_b, pbuf_b):
        half_d = d // 2

        yblk = []
        for j in range(n_slab):
            for pr in range(n_pair):
                cols = slice(pr * tile, (pr + 1) * tile)
                yb = jnp.dot(ublk_b[j, :, cols], toep_ref[j, pr], preferred_element_type=F32)
                yblk.append(yb + jnp.dot(w_b[j, :, cols].astype(BF16), ca_ref[j, pr],
                                         preferred_element_type=F32))
        h = _rms_norm(x_ref[...], gmix_ref[...])
        h = h * (1.0 + sc_ref[...][:, None, :]) + sh_ref[...][:, None, :]
        hb = h.reshape(nb * tt, d).astype(BF16)
        for j in range(n_slab):
            for pr in range(n_pair):
                yblk_ref[j, :, pr * tile:(pr + 1) * tile] = yblk[j * n_pair + pr]

        z_u = jnp.dot(hb, win_ref[:, :ssm_w], preferred_element_type=F32)
        for j in range(n_slab):
            groups = [yblk_ref[j, :, g * V7X_LANES:(g + 1) * V7X_LANES] for g in range(2 * n_pair)]
            steps = _granule_transpose(groups)
            y4 = jnp.stack([v.reshape(nk, nb, V7X_LANES) for v in steps], axis=1)
            ytm_ref[j] = y4.reshape(tt * nb, V7X_LANES)
        for b in range(nb):
            for j in range(n_slab):
                utm_ref[j, pl.ds(b, tt, stride=nb), :] = (
                    z_u[b * tt:(b + 1) * tt, j * V7X_LANES:(j + 1) * V7X_LANES])

        z_p = jnp.dot(hb, win_ref[:, ssm_w:], preferred_element_type=F32)
        y_nat = jnp.concatenate(
            [jnp.concatenate([ytm_ref[j, pl.ds(b, tt, stride=nb), :] for j in range(n_slab)], axis=1)
             for b in range(nb)], axis=0)
        yg = _gelu_tanh(y_nat).astype(BF16)
        pbuf_f[:, POOL_PAD + POOL_HALO:, :] = z_p.reshape(nb, tt, pool_w)

        lo, hi = POOL_PAD, POOL_PAD + POOL_HALO + tt
        pbuf_b[:, lo:lo + POOL_HALO, :] = jnp.where(first_back, 0.0, halo_ref[...])
        halo_ref[...] = pbuf_b[:, hi - POOL_HALO:hi, :]
        pos = (i_back * tt + 1 + lax.broadcasted_iota(jnp.int32, (1, tt, 1), 1)).astype(F32)

        def pooled(gi):
            win = POOL_WINDOWS[gi]
            cols = slice(gi * pgw, (gi + 1) * pgw)
            tok = pbuf_b[:, lo + POOL_HALO:hi, cols]
            sums = pbuf_b[:, lo:hi, cols] + pbuf_b[:, lo - 1:hi - 1, cols]
            span, level = 2, 0
            while span < win:
                lv = lvl_ref.at[level % 2]
                lv[:, lo:hi, :] = sums
                sums = sums + lv[:, lo - span:hi - span, :]
                span, level = 2 * span, level + 1
            zc = sums[:, POOL_HALO:, :] * (1.0 / jnp.minimum(pos, float(win))) - tok
            return zc.reshape(nb * tt, pgw).astype(BF16)

        n_pg = len(POOL_WINDOWS)
        t_val = jnp.dot(yg, wglu_ref[:, :ssm_w], preferred_element_type=F32)
        zc = [pooled(gi) for gi in range(n_pg // 2)]
        t_gate = jnp.dot(yg, wglu_ref[:, ssm_w:], preferred_element_type=F32)
        zc += [pooled(gi) for gi in range(n_pg // 2, n_pg)]
        y_ssm = ((t_val + bglu_ref[:, :ssm_w])
                 * jax.nn.sigmoid(t_gate + bglu_ref[:, ssm_w:])).astype(BF16)

        zp = [jnp.dot(zc[gi], wpool_ref[gi], preferred_element_type=F32) for gi in range(n_pg)]
        ubs = []
        for j in range(n_slab):
            u4 = utm_ref[j].reshape(nk, SSM_BLOCK, nb, V7X_LANES)
            steps = [u4[:, q].reshape(m, V7X_LANES) for q in range(SSM_BLOCK)]
            ub = jnp.concatenate(_granule_transpose(steps), axis=1).astype(BF16)
            ublk_f[j] = ub
            ubs.append(ub)
        w_taps = []
        for j in range(n_slab):
            for pr in range(n_pair):
                wp = jnp.dot(ubs[j][:, pr * tile:(pr + 1) * tile], ba_ref[j, pr],
                             preferred_element_type=F32)
                w_f[j, :, pr * tile:(pr + 1) * tile] = wp
                w_taps.append(wp[:, -V7X_LANES:])
        bpool = bpool_ref[...] + _zero_row(pool_w, *w_taps)
        y_pool = jnp.concatenate(
            [(zp[gi] + bpool[:, gi * pgw:(gi + 1) * pgw]) * pscale_ref[:, gi * pgw:(gi + 1) * pgw]
             for gi in range(n_pg)], axis=-1).astype(BF16)

        for c in range(2):
            cs = slice(c * half_d, (c + 1) * half_d)
            o_ref[:, :, cs] = jnp.dot(y_ssm, wout_ref[:ssm_w, cs],
                                      preferred_element_type=F32).reshape(nb, tt, half_d)

        chains = [(j, pr) for j in range(n_slab) for pr in range(n_pair)]
        state = [(jnp.where(first_front, 0.0, state_ref[j, pr, 0]),
                  jnp.where(first_front, 0.0, state_ref[j, pr, 1])) for j, pr in chains]
        for k in range(nk):
            rows = slice(k * nb, (k + 1) * nb)
            for ci, (j, pr) in enumerate(chains):
                c_re = slice(pr * tile, pr * tile + V7X_LANES)
                c_im = slice(pr * tile + V7X_LANES, (pr + 1) * tile)
                a_re, a_im = lam_ref[j, pr, 0], lam_ref[j, pr, 1]
                s_re, s_im = state[ci]
                w_re, w_im = w_f[j, rows, c_re], w_f[j, rows, c_im]
                w_f[j, rows, c_re] = s_re
                w_f[j, rows, c_im] = s_im
                state[ci] = (a_re * s_re - a_im * s_im + w_re, a_re * s_im + a_im * s_re + w_im)
        for (j, pr), (s_re, s_im) in zip(chains, state):
            state_ref[j, pr, 0] = s_re
            state_ref[j, pr, 1] = s_im

        for c in range(2):
            cs = slice(c * half_d, (c + 1) * half_d)
            mixed = jnp.dot(y_pool, wout_ref[ssm_w:, cs], preferred_element_type=F32)
            o_ref[:, :, cs] = gt_ref[:, cs][:, None, :] * (o_ref[:, :, cs]
                                                           + mixed.reshape(nb, tt, half_d))

    @pl.when(s % 2 == 0)
    def _():
        step(ublk0, w0, pbuf0, ublk1, w1, pbuf1)

    @pl.when(s % 2 == 1)
    def _():
        step(ublk1, w1, pbuf1, ublk0, w0, pbuf0)


def _mixer(x, mod, g_norm, w_in_bf, toep, ba, ca, lam_t, w_glu_bf, b_glu, w_pool_bf, b_pool, pool_scale,
           w_out_bf, *, tt):
    bsz, seq, d = x.shape
    nb = BATCH_GROUP
    n_slab, n_pair = toep.shape[0], toep.shape[1]
    ssm_w = n_slab * V7X_LANES
    pool_w = w_in_bf.shape[1] - ssm_w
    n_tiles = seq // tt
    n_total = (bsz // nb) * n_tiles
    m = tt // SSM_BLOCK * nb
    width = n_pair * 2 * V7X_LANES

    def front(s):
        t = jnp.minimum(s, n_total - 1)
        return t // n_tiles, t % n_tiles

    def back(s):
        t = jnp.maximum(s - 1, 0)
        return t // n_tiles, t % n_tiles

    kern = functools.partial(_mixer_kernel, tt=tt, n_tiles=n_tiles, n_total=n_total)
    return pl.pallas_call(
        kern,
        grid=(n_total + 1,),
        in_specs=[
            pl.BlockSpec((nb, tt, d), lambda s: (*front(s), 0)),
            pl.BlockSpec((nb, d), lambda s: (front(s)[0], 0)),
            pl.BlockSpec((nb, d), lambda s: (front(s)[0], 1)),
            pl.BlockSpec((nb, d), lambda s: (back(s)[0], 2)),
            _const_spec((1, d)),
            _const_spec(w_in_bf.shape),
            _const_spec(toep.shape),
            _const_spec(ba.shape),
            _const_spec(ca.shape),
            _const_spec(lam_t.shape),
            _const_spec(w_glu_bf.shape),
            _const_spec((1, 2 * ssm_w)),
            _const_spec(w_pool_bf.shape),
            _const_spec((1, pool_w)),
            _const_spec((1, pool_w)),
            _const_spec(w_out_bf.shape),
        ],
        out_specs=pl.BlockSpec((nb, tt, d), lambda s: (*back(s), 0)),
        out_shape=jax.ShapeDtypeStruct(x.shape, F32),
        scratch_shapes=[
            pltpu.VMEM((n_slab, tt * nb, V7X_LANES), F32),
            pltpu.VMEM((n_slab, tt * nb, V7X_LANES), F32),
            pltpu.VMEM((n_slab, m, width), F32),
            pltpu.VMEM((n_slab, n_pair, 2, nb, V7X_LANES), F32),
            pltpu.VMEM((nb, POOL_HALO, pool_w), F32),
            pltpu.VMEM((2, nb, POOL_PAD + POOL_HALO + tt, pool_w // len(POOL_WINDOWS)), F32),
            pltpu.VMEM((n_slab, m, width), BF16),
            pltpu.VMEM((n_slab, m, width), BF16),
            pltpu.VMEM((n_slab, m, width), F32),
            pltpu.VMEM((n_slab, m, width), F32),
            pltpu.VMEM((nb, POOL_PAD + POOL_HALO + tt, pool_w), F32),
            pltpu.VMEM((nb, POOL_PAD + POOL_HALO + tt, pool_w), F32),
        ],
        compiler_params=pltpu.CompilerParams(
            dimension_semantics=("arbitrary",),
            vmem_limit_bytes=V7X_VMEM_LIMIT_BYTES),
        name="mixer",
    )(x, mod, mod, mod, g_norm.reshape(1, d), w_in_bf, toep, ba, ca, lam_t, w_glu_bf,
      b_glu.reshape(1, -1), w_pool_bf, b_pool.reshape(1, -1), pool_scale.reshape(1, -1), w_out_bf)


FFN_ROW_SLICES = 8
FFN_UP_AHEAD = 2


def _ffn_kernel(xf_ref, dlf_ref, sh_ref, sc_ref, xb_ref, dlb_ref, gt_ref, gn_ref, wup_ref, wconv_ref,
                bconv_ref, wdown_ref, gfin_ref, o_ref, gbuf_ref, gprev_ref, h0, h1, acc0, acc1,
                *, tm, fc, n_tiles):
    s = pl.program_id(0)
    first_mid = jnp.maximum(s - 1, 0) % n_tiles == 0
    d_ff = wdown_ref.shape[0]
    n_chunks = d_ff // fc
    n_gbuf = gbuf_ref.shape[0]
    rs = tm // FFN_ROW_SLICES

    @pl.when(s == 0)
    def _():
        h1[...] = jnp.zeros_like(h1)
        acc0[...] = jnp.zeros_like(acc0)
        gprev_ref[...] = jnp.zeros_like(gprev_ref)

    def step(h_f, h_m, acc_m, acc_b):
        def up(c):
            cols = slice(c * fc, (c + 1) * fc)
            gb = gbuf_ref.at[c % n_gbuf]
            h = h_m[...]
            v = jnp.dot(h, wup_ref[:, c * fc:(c + 1) * fc], preferred_element_type=F32)
            g = jnp.dot(h, wup_ref[:, d_ff + c * fc:d_ff + (c + 1) * fc],
                        preferred_element_type=F32)
            gb[0:CONV_HALO, :] = jnp.where(first_mid, 0.0, gprev_ref[:, cols])
            gb[CONV_HALO:, :] = g
            gprev_ref[:, cols] = g[tm - CONV_HALO:, :]
            return v

        pending = {c: up(c) for c in range(FFN_UP_AHEAD)}
        anchor = None
        for c in range(n_chunks):
            cols = slice(c * fc, (c + 1) * fc)
            if c + FFN_UP_AHEAD < n_chunks:
                pending[c + FFN_UP_AHEAD] = up(c + FFN_UP_AHEAD)
            v = pending.pop(c)
            bias = bconv_ref[:, cols]
            if anchor is not None:
                bias = bias + anchor
            gb = gbuf_ref.at[c % n_gbuf]
            gc = bias + wconv_ref[CONV_WIDTH - 1:CONV_WIDTH, cols] * gb[CONV_HALO:, :]
            for k in range(1, CONV_WIDTH):
                gc = gc + (wconv_ref[CONV_WIDTH - 1 - k:CONV_WIDTH - k, cols]
                           * gb[CONV_HALO - k:CONV_HALO - k + tm, :])
            a = (gc * jax.nn.sigmoid(gc) * v).astype(BF16)
            part = jnp.dot(a, wdown_ref[cols, :], preferred_element_type=F32)
            if c == 0:
                acc_m[...] = part
            else:
                acc_m[...] += part
            anchor = None
            if c < FFN_ROW_SLICES:
                rows = slice(c * rs, (c + 1) * rs)
                y = xb_ref[0, rows, :] + dlb_ref[0, rows, :] + gt_ref[0, 0] * acc_b[rows, :]
                y = _rms_norm(y, gfin_ref[...])
                o_ref[0, rows, :] = y
                hf = _rms_norm(xf_ref[0, rows, :] + dlf_ref[0, rows, :], gn_ref[...])
                hf = hf * (1.0 + sc_ref[0, 0]) + sh_ref[0, 0]
                h_f[rows, :] = hf.astype(BF16)
                anchor = _zero_row(fc, y, hf)

    @pl.when(s % 2 == 0)
    def _():
        step(h0, h1, acc1, acc0)

    @pl.when(s % 2 == 1)
    def _():
        step(h1, h0, acc0, acc1)


def _ffn(x, delta, mod, g_norm, w_up_bf, w_conv, b_conv, w_down_bf, g_final, *, tm, fc):
    bsz, seq, d = x.shape
    d_ff = w_down_bf.shape[0]
    n_tiles = seq // tm
    n_total = bsz * n_tiles
    assert d_ff % fc == 0 and d_ff // fc >= FFN_ROW_SLICES and tm % (FFN_ROW_SLICES * V7X_SUBLANES) == 0
    mod4 = mod.reshape(bsz, N_MOD, 1, d)

    def front(s):
        t = jnp.minimum(s, n_total - 1)
        return t // n_tiles, t % n_tiles

    def back(s):
        t = jnp.maximum(s - 2, 0)
        return t // n_tiles, t % n_tiles

    kern = functools.partial(_ffn_kernel, tm=tm, fc=fc, n_tiles=n_tiles)
    return pl.pallas_call(
        kern,
        grid=(n_total + 2,),
        in_specs=[
            pl.BlockSpec((1, tm, d), lambda s: (*front(s), 0)),
            pl.BlockSpec((1, tm, d), lambda s: (*front(s), 0)),
            pl.BlockSpec((1, 1, 1, d), lambda s: (front(s)[0], 3, 0, 0)),
            pl.BlockSpec((1, 1, 1, d), lambda s: (front(s)[0], 4, 0, 0)),
            pl.BlockSpec((1, tm, d), lambda s: (*back(s), 0)),
            pl.BlockSpec((1, tm, d), lambda s: (*back(s), 0)),
            pl.BlockSpec((1, 1, 1, d), lambda s: (back(s)[0], 5, 0, 0)),
            _const_spec((1, d)),
            _const_spec(w_up_bf.shape),
            _const_spec(w_conv.shape),
            _const_spec((1, d_ff)),
            _const_spec(w_down_bf.shape),
            _const_spec((1, d)),
        ],
        out_specs=pl.BlockSpec((1, tm, d), lambda s: (*back(s), 0)),
        out_shape=jax.ShapeDtypeStruct(x.shape, F32),
        scratch_shapes=[
            pltpu.VMEM((FFN_UP_AHEAD + 1, CONV_HALO + tm, fc), F32),
            pltpu.VMEM((CONV_HALO, d_ff), F32),
            pltpu.VMEM((tm, d), BF16),
            pltpu.VMEM((tm, d), BF16),
            pltpu.VMEM((tm, d), F32),
            pltpu.VMEM((tm, d), F32),
        ],
        compiler_params=pltpu.CompilerParams(
            dimension_semantics=("arbitrary",),
            vmem_limit_bytes=V7X_VMEM_LIMIT_BYTES),
        name="ffn",
    )(x, delta, mod4, mod4, x, delta, mod4, g_norm.reshape(1, d), w_up_bf, w_conv,
      b_conv.reshape(1, d_ff), w_down_bf, g_final.reshape(1, d))


def _ssm_params(lam_re, lam_im, log_dt, b_re, b_im, c_re, c_im, d_skip):
    n_groups, n_state = lam_re.shape
    h, t = SSM_GROUP, SSM_BLOCK
    th = t * h
    gps = V7X_LANES // h
    n_slab, n_pair = n_groups // gps, gps // 2
    lam_re, lam_im = lam_re.astype(F32), lam_im.astype(F32)
    dt = jnp.exp(log_dt.astype(F32))[:, None]

    k = jnp.arange(t + 1, dtype=F32)[None, :, None]
    mag = jnp.exp(k * (dt * lam_re)[:, None, :])
    ang = k * (dt * lam_im)[:, None, :]
    p_re, p_im = mag * jnp.cos(ang), mag * jnp.sin(ang)

    a_re, a_im = p_re[:, 1], p_im[:, 1]
    den = lam_re * lam_re + lam_im * lam_im
    q_re = (((a_re - 1.0) * lam_re + a_im * lam_im) / den)[:, None, :]
    q_im = ((a_im * lam_re - (a_re - 1.0) * lam_im) / den)[:, None, :]
    b_re = b_re.astype(F32).transpose(0, 2, 1)
    b_im = b_im.astype(F32).transpose(0, 2, 1)
    bb_re = q_re * b_re - q_im * b_im
    bb_im = q_re * b_im + q_im * b_re
    c_re, c_im = c_re.astype(F32)[:, None], c_im.astype(F32)[:, None]

    pk_re, pk_im = p_re[:, :, None, :], p_im[:, :, None, :]
    cak = jnp.concatenate([c_re * pk_re - c_im * pk_im, -(c_re * pk_im + c_im * pk_re)], axis=-1)
    cat0 = cak[:, :t].reshape(n_groups, th, 2 * n_state)
    cat = cak[:, 1:].reshape(n_groups, th, 2 * n_state)
    r_re, r_im = p_re[:, t - 1::-1][:, :, None, :], p_im[:, t - 1::-1][:, :, None, :]
    ba = jnp.concatenate([r_re * bb_re[:, None] - r_im * bb_im[:, None],
                          r_re * bb_im[:, None] + r_im * bb_re[:, None]], axis=-1)
    ba = ba.reshape(n_groups, th, 2 * n_state)
    bmat = jnp.concatenate([bb_re, bb_im], axis=-1)
    kexp = jnp.einsum('ghk,gnk->ghn', bmat, cat0)
    kexp = kexp + d_skip.astype(F32).reshape(n_groups, h)[:, :, None] * jnp.eye(h, th, dtype=F32)
    toep = jnp.stack([jnp.pad(kexp, ((0, 0), (0, 0), (h * si, 0)))[:, :, :th] for si in range(t)],
                     axis=1).reshape(n_groups, th, th)

    def pair_diag(m):
        m = m.reshape(n_slab, n_pair, 2, th, th)
        z = jnp.zeros_like(m[:, :, 0])
        return jnp.concatenate([jnp.concatenate([m[:, :, 0], z], axis=-1),
                                jnp.concatenate([z, m[:, :, 1]], axis=-1)], axis=-2)

    def pair_state(m):
        m = m.reshape(n_slab, n_pair, 2, th, 2, n_state)
        z = jnp.zeros_like(m[:, :, 0, :, 0])
        top = jnp.concatenate([m[:, :, 0, :, 0], z, m[:, :, 0, :, 1], z], axis=-1)
        bot = jnp.concatenate([z, m[:, :, 1, :, 0], z, m[:, :, 1, :, 1]], axis=-1)
        return jnp.concatenate([top, bot], axis=-2)

    lam_t = jnp.stack([p_re[:, t].reshape(n_slab, n_pair, 2 * n_state),
                       p_im[:, t].reshape(n_slab, n_pair, 2 * n_state)], axis=2)
    lam_t = jnp.broadcast_to(lam_t[:, :, :, None, :], (n_slab, n_pair, 2, BATCH_GROUP, 2 * n_state))
    ca = pair_state(cat).astype(BF16).swapaxes(2, 3)
    return pair_diag(toep).astype(BF16), pair_state(ba).astype(BF16), ca, lam_t


def _layer(x, c, w_ada, b_ada, g_norm_mix, w_in, ssm_lam_re, ssm_lam_im, ssm_log_dt, ssm_b_re,
           ssm_b_im, ssm_c_re, ssm_c_im, ssm_d, w_glu, b_glu, w_pool, b_pool, pool_scale, w_out,
           g_norm_ffn, w_up, w_conv, b_conv, w_down, g_final):
    bsz, seq, d = x.shape
    ssm_w = ssm_d.shape[0]
    tt, tm_ffn, fc = _tile_sizes(seq)
    assert bsz % BATCH_GROUP == 0 and ssm_w % V7X_LANES == 0
    assert seq % tt == 0 and seq % tm_ffn == 0 and tt >= POOL_HALO and tt % SSM_BLOCK == 0

    mod = _adaln(c, w_ada, b_ada)
    toep, ba, ca, lam_t = _ssm_params(ssm_lam_re, ssm_lam_im, ssm_log_dt, ssm_b_re, ssm_b_im,
                                      ssm_c_re, ssm_c_im, ssm_d)
    delta = _mixer(x, mod, g_norm_mix, w_in.astype(BF16), toep, ba, ca, lam_t, w_glu.astype(BF16),
                   b_glu, w_pool.astype(BF16), b_pool, pool_scale, w_out.astype(BF16), tt=tt)
    return _ffn(x, delta, mod, g_norm_ffn, w_up.astype(BF16), w_conv, b_conv, w_down.astype(BF16),
                g_final, tm=tm_ffn, fc=fc)


def kernel(x, c, w_ada, b_ada, g_norm_mix, w_in, ssm_lam_re, ssm_lam_im, ssm_log_dt, ssm_b_re,
           ssm_b_im, ssm_c_re, ssm_c_im, ssm_d, w_glu, b_glu, w_pool, b_pool, pool_scale, w_out,
           g_norm_ffn, w_up, w_conv, b_conv, w_down, g_norm_final):
    assert w_ada.shape[0] == 1, "single-layer block"
    return _layer(x, c, w_ada[0], b_ada[0], g_norm_mix[0], w_in[0], ssm_lam_re[0], ssm_lam_im[0],
                  ssm_log_dt[0], ssm_b_re[0], ssm_b_im[0], ssm_c_re[0], ssm_c_im[0], ssm_d[0],
                  w_glu[0], b_glu[0], w_pool[0], b_pool[0], pool_scale[0], w_out[0],
                  g_norm_ffn[0], w_up[0], w_conv[0], b_conv[0], w_down[0], g_norm_final)
```

```python
import functools

import jax
import jax.numpy as jnp
from jax import lax
from jax.experimental import pallas as pl
from jax.experimental.pallas import tpu as pltpu

V7X_LANES = 128
V7X_SUBLANES = 8
V7X_VMEM_LIMIT_BYTES = 56 * 1024 * 1024

BATCH_GROUP = V7X_SUBLANES

SSM_GROUP = 16
SSM_STATE = 64
SSM_BLOCK = 8
POOL_WINDOWS = (2, 4, 8, 16)
POOL_HALO = 16
POOL_PAD = 8
CONV_WIDTH = 3
CONV_HALO = 8
N_MOD = 6
EPS = 1e-6

F32 = jnp.float32
BF16 = jnp.bfloat16


def _tile_sizes(seq):
    return min(128, seq), min(512, seq), 256


def _const_spec(shape):
    zeros = (0,) * len(shape)
    return pl.BlockSpec(shape, lambda *_: zeros, pipeline_mode=pl.Buffered(1))


def _rms_norm(x, g):
    ms = jnp.mean(x * x, axis=-1, keepdims=True)
    return x * lax.rsqrt(ms + EPS) * g


def _zero_row(width, *arrays):
    acc = None
    for a in arrays:
        bits = lax.bitcast_convert_type(a.astype(F32), jnp.int32)
        rows, cols = bits.shape
        blocks = [bits[r:r + V7X_SUBLANES] for r in range(0, rows, V7X_SUBLANES)]
        bits = functools.reduce(jnp.bitwise_or, blocks)
        tiles = [bits[:, c:c + V7X_LANES] for c in range(0, cols, V7X_LANES)]
        bits = functools.reduce(jnp.bitwise_or, tiles)
        acc = bits if acc is None else acc | bits
    zero = lax.shift_right_logical(lax.shift_right_logical(acc, 16), 16).astype(F32)
    return jnp.concatenate([zero[0:1]] * (width // V7X_LANES), axis=1)


def _adaln_kernel(c_ref, w_ref, b_ref, o_ref):
    c = c_ref[...]
    s = c * jax.nn.sigmoid(c)
    o_ref[...] = jnp.dot(s.astype(BF16), w_ref[...].astype(BF16),
                         preferred_element_type=F32) + b_ref[...]


def _adaln(c, w_ada, b_ada):
    bsz, d = c.shape
    n = w_ada.shape[1]
    return pl.pallas_call(
        _adaln_kernel,
        grid=(n // d,),
        in_specs=[
            pl.BlockSpec((bsz, d), lambda k: (0, 0)),
            pl.BlockSpec((d, d), lambda k: (0, k)),
            pl.BlockSpec((1, d), lambda k: (0, k)),
        ],
        out_specs=pl.BlockSpec((bsz, d), lambda k: (0, k)),
        out_shape=jax.ShapeDtypeStruct((bsz, n), F32),
        name="adaln",
    )(c, w_ada, b_ada.reshape(1, n))


def _gelu_tanh(x, one=1.0):
    return 0.5 * x * (one + jnp.tanh(0.7978845608028654 * (x + 0.044715 * (x * x * x))))


def _granule_transpose(v):
    granule = lax.broadcasted_iota(jnp.int32, v[0].shape, 1) // SSM_GROUP
    v = list(v)
    for d in (4, 2, 1):
        hi = (granule & d) != 0
        nxt = list(v)
        for i in range(len(v)):
            if i & d:
                continue
            lo_v, hi_v = v[i], v[i + d]
            nxt[i] = jnp.where(hi, pltpu.roll(hi_v, SSM_GROUP * d, axis=1), lo_v)
            nxt[i + d] = jnp.where(hi, hi_v, pltpu.roll(lo_v, V7X_LANES - SSM_GROUP * d, axis=1))
        v = nxt
    return v


def _mixer_kernel(x_ref, sh_ref, sc_ref, gt_ref, gmix_ref, win_ref, toep_ref, ba_ref, ca_ref, lam_ref,
                  wglu_ref, bglu_ref, wpool_ref, bpool_ref, pscale_ref, wout_ref, wup_ref, wdown_ref,
                  o_ref, wup_bf_ref, wdown_bf_ref, utm_ref, ytm_ref, yblk_ref, state_ref, halo_ref, lvl_ref, ublk0, ublk1, w0, w1, pbuf0,
                  pbuf1,
                  *, tt, n_tiles, n_total):
    s = pl.program_id(0)
    first_front = jnp.minimum(s, n_total - 1) % n_tiles == 0
    i_back = jnp.maximum(s - 1, 0) % n_tiles
    first_back = i_back == 0
    nb, _, d = x_ref.shape
    n_slab, n_pair = toep_ref.shape[0], toep_ref.shape[1]
    ssm_w = n_slab * V7X_LANES
    pool_w = halo_ref.shape[2]
    pgw = pool_w // len(POOL_WINDOWS)
    nk = tt // SSM_BLOCK
    m = nk * nb
    tile = 2 * V7X_LANES

    wup_bf_ref[...] = wup_ref[...].astype(BF16)
    wdown_bf_ref[...] = wdown_ref[...].astype(BF16)

    @pl.when(s == 0)
    def _():
        ublk1[...] = jnp.zeros_like(ublk1)
        w1[...] = jnp.zeros_like(w1)
        pbuf1[...] = jnp.zeros_like(pbuf1)
        pbuf0[:, 0:POOL_PAD, :] = jnp.zeros((nb, POOL_PAD, pool_w), F32)
        lvl_ref[...] = jnp.zeros_like(lvl_ref)
        halo_ref[...] = jnp.zeros_like(halo_ref)
        state_ref[...] = jnp.zeros_like(state_ref)

    def step(ublk_f, w_f, pbuf_f, ublk_b, w_b, pbuf_b):
        half_d = d // 2

        yblk = []
        for j in range(n_slab):
            for pr in range(n_pair):
                cols = slice(pr * tile, (pr + 1) * tile)
                yb = jnp.dot(ublk_b[j, :, cols], toep_ref[j, pr], preferred_element_type=F32)
                yblk.append(yb + jnp.dot(w_b[j, :, cols].astype(BF16), ca_ref[j, pr],
                                         preferred_element_type=F32))
        h = _rms_norm(x_ref[...], gmix_ref[...])
        h = h * (1.0 + sc_ref[...][:, None, :]) + sh_ref[...][:, None, :]
        hb = h.reshape(nb * tt, d).astype(BF16)
        for j in range(n_slab):
            for pr in range(n_pair):
                yblk_ref[j, :, pr * tile:(pr + 1) * tile] = yblk[j * n_pair + pr]

        z_u = jnp.dot(hb, win_ref[:, :ssm_w], preferred_element_type=F32)
        for j in range(n_slab):
            groups = [yblk_ref[j, :, g * V7X_LANES:(g + 1) * V7X_LANES] for g in range(2 * n_pair)]
            steps = _granule_transpose(groups)
            y4 = jnp.stack([v.reshape(nk, nb, V7X_LANES) for v in steps], axis=1)
            ytm_ref[j] = y4.reshape(tt * nb, V7X_LANES)
        for b in range(nb):
            for j in range(n_slab):
                utm_ref[j, pl.ds(b, tt, stride=nb), :] = (
                    z_u[b * tt:(b + 1) * tt, j * V7X_LANES:(j + 1) * V7X_LANES])

        z_p = jnp.dot(hb, win_ref[:, ssm_w:], preferred_element_type=F32)
        y_nat = jnp.concatenate(
            [jnp.concatenate([ytm_ref[j, pl.ds(b, tt, stride=nb), :] for j in range(n_slab)], axis=1)
             for b in range(nb)], axis=0)
        yg = _gelu_tanh(y_nat).astype(BF16)
        pbuf_f[:, POOL_PAD + POOL_HALO:, :] = z_p.reshape(nb, tt, pool_w)

        lo, hi = POOL_PAD, POOL_PAD + POOL_HALO + tt
        pbuf_b[:, lo:lo + POOL_HALO, :] = jnp.where(first_back, 0.0, halo_ref[...])
        halo_ref[...] = pbuf_b[:, hi - POOL_HALO:hi, :]
        pos = (i_back * tt + 1 + lax.broadcasted_iota(jnp.int32, (1, tt, 1), 1)).astype(F32)

        def pooled(gi):
            win = POOL_WINDOWS[gi]
            cols = slice(gi * pgw, (gi + 1) * pgw)
            tok = pbuf_b[:, lo + POOL_HALO:hi, cols]
            sums = pbuf_b[:, lo:hi, cols] + pbuf_b[:, lo - 1:hi - 1, cols]
            span, level = 2, 0
            while span < win:
                lv = lvl_ref.at[level % 2]
                lv[:, lo:hi, :] = sums
                sums = sums + lv[:, lo - span:hi - span, :]
                span, level = 2 * span, level + 1
            zc = sums[:, POOL_HALO:, :] * (1.0 / jnp.minimum(pos, float(win))) - tok
            return zc.reshape(nb * tt, pgw).astype(BF16)

        n_pg = len(POOL_WINDOWS)
        t_val = jnp.dot(yg, wglu_ref[:, :ssm_w], preferred_element_type=F32)
        zc = [pooled(gi) for gi in range(n_pg // 2)]
        t_gate = jnp.dot(yg, wglu_ref[:, ssm_w:], preferred_element_type=F32)
        zc += [pooled(gi) for gi in range(n_pg // 2, n_pg)]
        y_ssm = ((t_val + bglu_ref[:, :ssm_w])
                 * jax.nn.sigmoid(t_gate + bglu_ref[:, ssm_w:])).astype(BF16)

        for c in range(2):
            cs = slice(c * half_d, (c + 1) * half_d)
            o_ref[:, :, cs] = jnp.dot(y_ssm, wout_ref[:ssm_w, cs],
                                      preferred_element_type=F32).reshape(nb, tt, half_d)

        ubs = []
        for j in range(n_slab):
            u4 = utm_ref[j].reshape(nk, SSM_BLOCK, nb, V7X_LANES)
            steps = [u4[:, q].reshape(m, V7X_LANES) for q in range(SSM_BLOCK)]
            ub = jnp.concatenate(_granule_transpose(steps), axis=1).astype(BF16)
            ublk_f[j] = ub
            ubs.append(ub)
        w_taps = []
        for j in range(n_slab):
            for pr in range(n_pair):
                wp = jnp.dot(ubs[j][:, pr * tile:(pr + 1) * tile], ba_ref[j, pr],
                             preferred_element_type=F32)
                w_f[j, :, pr * tile:(pr + 1) * tile] = wp
                w_taps.append(wp[:, -V7X_LANES:])
        zp = [jnp.dot(zc[gi], wpool_ref[gi], preferred_element_type=F32) for gi in range(n_pg)]
        bpool = bpool_ref[...] + _zero_row(pool_w, *w_taps)
        y_pool = jnp.concatenate(
            [(zp[gi] + bpool[:, gi * pgw:(gi + 1) * pgw]) * pscale_ref[:, gi * pgw:(gi + 1) * pgw]
             for gi in range(n_pg)], axis=-1).astype(BF16)

        chains = [(j, pr) for j in range(n_slab) for pr in range(n_pair)]
        state = [(jnp.where(first_front, 0.0, state_ref[j, pr, 0]),
                  jnp.where(first_front, 0.0, state_ref[j, pr, 1])) for j, pr in chains]
        for k in range(nk):
            rows = slice(k * nb, (k + 1) * nb)
            for ci, (j, pr) in enumerate(chains):
                c_re = slice(pr * tile, pr * tile + V7X_LANES)
                c_im = slice(pr * tile + V7X_LANES, (pr + 1) * tile)
                a_re, a_im = lam_ref[j, pr, 0], lam_ref[j, pr, 1]
                s_re, s_im = state[ci]
                w_re, w_im = w_f[j, rows, c_re], w_f[j, rows, c_im]
                w_f[j, rows, c_re] = s_re
                w_f[j, rows, c_im] = s_im
                state[ci] = (a_re * s_re - a_im * s_im + w_re, a_re * s_im + a_im * s_re + w_im)
        for (j, pr), (s_re, s_im) in zip(chains, state):
            state_ref[j, pr, 0] = s_re
            state_ref[j, pr, 1] = s_im

        for c in range(2):
            cs = slice(c * half_d, (c + 1) * half_d)
            mixed = jnp.dot(y_pool, wout_ref[ssm_w:, cs], preferred_element_type=F32)
            o_ref[:, :, cs] = gt_ref[:, cs][:, None, :] * (o_ref[:, :, cs]
                                                           + mixed.reshape(nb, tt, half_d))

    @pl.when(s % 2 == 0)
    def _():
        step(ublk0, w0, pbuf0, ublk1, w1, pbuf1)

    @pl.when(s % 2 == 1)
    def _():
        step(ublk1, w1, pbuf1, ublk0, w0, pbuf0)


def _cast_block_rows(rows, n_steps):
    tile = 2 * V7X_SUBLANES
    return next(r for r in range(tile, rows + 1, tile) if rows % r == 0 and rows // r <= n_steps)


def _mixer(x, mod, g_norm, w_in_bf, toep, ba, ca, lam_t, w_glu_bf, b_glu, w_pool_bf, b_pool, pool_scale,
           w_out_bf, w_up, w_down, *, tt):
    bsz, seq, d = x.shape
    nb = BATCH_GROUP
    n_slab, n_pair = toep.shape[0], toep.shape[1]
    ssm_w = n_slab * V7X_LANES
    pool_w = w_in_bf.shape[1] - ssm_w
    n_tiles = seq // tt
    n_total = (bsz // nb) * n_tiles
    m = tt // SSM_BLOCK * nb
    width = n_pair * 2 * V7X_LANES

    def front(s):
        t = jnp.minimum(s, n_total - 1)
        return t // n_tiles, t % n_tiles

    def back(s):
        t = jnp.maximum(s - 1, 0)
        return t // n_tiles, t % n_tiles

    up_rows = _cast_block_rows(w_up.shape[0], n_total)
    down_rows = _cast_block_rows(w_down.shape[0], n_total)
    up_spec = pl.BlockSpec((up_rows, w_up.shape[1]),
                           lambda s: (jnp.minimum(s, w_up.shape[0] // up_rows - 1), 0))
    down_spec = pl.BlockSpec((down_rows, w_down.shape[1]),
                             lambda s: (jnp.minimum(s, w_down.shape[0] // down_rows - 1), 0))

    kern = functools.partial(_mixer_kernel, tt=tt, n_tiles=n_tiles, n_total=n_total)
    return pl.pallas_call(
        kern,
        grid=(n_total + 1,),
        in_specs=[
            pl.BlockSpec((nb, tt, d), lambda s: (*front(s), 0)),
            pl.BlockSpec((nb, d), lambda s: (front(s)[0], 0)),
            pl.BlockSpec((nb, d), lambda s: (front(s)[0], 1)),
            pl.BlockSpec((nb, d), lambda s: (back(s)[0], 2)),
            _const_spec((1, d)),
            _const_spec(w_in_bf.shape),
            _const_spec(toep.shape),
            _const_spec(ba.shape),
            _const_spec(ca.shape),
            _const_spec(lam_t.shape),
            _const_spec(w_glu_bf.shape),
            _const_spec((1, 2 * ssm_w)),
            _const_spec(w_pool_bf.shape),
            _const_spec((1, pool_w)),
            _const_spec((1, pool_w)),
            _const_spec(w_out_bf.shape),
            up_spec,
            down_spec,
        ],
        out_specs=[pl.BlockSpec((nb, tt, d), lambda s: (*back(s), 0)), up_spec, down_spec],
        out_shape=[jax.ShapeDtypeStruct(x.shape, F32), jax.ShapeDtypeStruct(w_up.shape, BF16),
                   jax.ShapeDtypeStruct(w_down.shape, BF16)],
        scratch_shapes=[
            pltpu.VMEM((n_slab, tt * nb, V7X_LANES), F32),
            pltpu.VMEM((n_slab, tt * nb, V7X_LANES), F32),
            pltpu.VMEM((n_slab, m, width), F32),
            pltpu.VMEM((n_slab, n_pair, 2, nb, V7X_LANES), F32),
            pltpu.VMEM((nb, POOL_HALO, pool_w), F32),
            pltpu.VMEM((2, nb, POOL_PAD + POOL_HALO + tt, pool_w // len(POOL_WINDOWS)), F32),
            pltpu.VMEM((n_slab, m, width), BF16),
            pltpu.VMEM((n_slab, m, width), BF16),
            pltpu.VMEM((n_slab, m, width), F32),
            pltpu.VMEM((n_slab, m, width), F32),
            pltpu.VMEM((nb, POOL_PAD + POOL_HALO + tt, pool_w), F32),
            pltpu.VMEM((nb, POOL_PAD + POOL_HALO + tt, pool_w), F32),
        ],
        compiler_params=pltpu.CompilerParams(
            dimension_semantics=("arbitrary",),
            vmem_limit_bytes=V7X_VMEM_LIMIT_BYTES),
        name="mixer",
    )(x, mod, mod, mod, g_norm.reshape(1, d), w_in_bf, toep, ba, ca, lam_t, w_glu_bf,
      b_glu.reshape(1, -1), w_pool_bf, b_pool.reshape(1, -1), pool_scale.reshape(1, -1), w_out_bf,
      w_up, w_down)


FFN_ROW_SLICES = 8
FFN_UP_AHEAD = 2


def _ffn_kernel(xf_ref, dlf_ref, sh_ref, sc_ref, xb_ref, dlb_ref, gt_ref, gn_ref, wup_ref, wconv_ref,
                bconv_ref, wdown_ref, gfin_ref, o_ref, gbuf_ref, gprev_ref, h0, h1, acc0, acc1,
                *, tm, fc, n_tiles):
    s = pl.program_id(0)
    first_mid = jnp.maximum(s - 1, 0) % n_tiles == 0
    d_ff = wdown_ref.shape[0]
    n_chunks = d_ff // fc
    n_gbuf = gbuf_ref.shape[0]
    rs = tm // FFN_ROW_SLICES

    @pl.when(s == 0)
    def _():
        h1[...] = jnp.zeros_like(h1)
        acc0[...] = jnp.zeros_like(acc0)
        gprev_ref[...] = jnp.zeros_like(gprev_ref)

    def step(h_f, h_m, acc_m, acc_b):
        def up(c):
            cols = slice(c * fc, (c + 1) * fc)
            gb = gbuf_ref.at[c % n_gbuf]
            h = h_m[...]
            v = jnp.dot(h, wup_ref[:, c * fc:(c + 1) * fc], preferred_element_type=F32)
            g = jnp.dot(h, wup_ref[:, d_ff + c * fc:d_ff + (c + 1) * fc],
                        preferred_element_type=F32)
            gb[0:CONV_HALO, :] = jnp.where(first_mid, 0.0, gprev_ref[:, cols])
            gb[CONV_HALO:, :] = g
            gprev_ref[:, cols] = g[tm - CONV_HALO:, :]
            return v

        pending = {c: up(c) for c in range(FFN_UP_AHEAD)}
        anchor = None
        for c in range(n_chunks):
            cols = slice(c * fc, (c + 1) * fc)
            if c + FFN_UP_AHEAD < n_chunks:
                pending[c + FFN_UP_AHEAD] = up(c + FFN_UP_AHEAD)
            v = pending.pop(c)
            bias = bconv_ref[:, cols]
            if anchor is not None:
                bias = bias + anchor
            gb = gbuf_ref.at[c % n_gbuf]
            gc = bias + wconv_ref[CONV_WIDTH - 1:CONV_WIDTH, cols] * gb[CONV_HALO:, :]
            for k in range(1, CONV_WIDTH):
                gc = gc + (wconv_ref[CONV_WIDTH - 1 - k:CONV_WIDTH - k, cols]
                           * gb[CONV_HALO - k:CONV_HALO - k + tm, :])
            a = (gc * jax.nn.sigmoid(gc) * v).astype(BF16)
            part = jnp.dot(a, wdown_ref[cols, :], preferred_element_type=F32)
            if c == 0:
                acc_m[...] = part
            else:
                acc_m[...] += part
            anchor = None
            if c < FFN_ROW_SLICES:
                rows = slice(c * rs, (c + 1) * rs)
                y = xb_ref[0, rows, :] + dlb_ref[0, rows, :] + gt_ref[0, 0] * acc_b[rows, :]
                y = _rms_norm(y, gfin_ref[...])
                o_ref[0, rows, :] = y
                hf = _rms_norm(xf_ref[0, rows, :] + dlf_ref[0, rows, :], gn_ref[...])
                hf = hf * (1.0 + sc_ref[0, 0]) + sh_ref[0, 0]
                h_f[rows, :] = hf.astype(BF16)
                anchor = _zero_row(fc, y, hf)

    @pl.when(s % 2 == 0)
    def _():
        step(h0, h1, acc1, acc0)

    @pl.when(s % 2 == 1)
    def _():
        step(h1, h0, acc0, acc1)


def _ffn(x, delta, mod, g_norm, w_up_bf, w_conv, b_conv, w_down_bf, g_final, *, tm, fc):
    bsz, seq, d = x.shape
    d_ff = w_down_bf.shape[0]
    n_tiles = seq // tm
    n_total = bsz * n_tiles
    assert d_ff % fc == 0 and d_ff // fc >= FFN_ROW_SLICES and tm % (FFN_ROW_SLICES * V7X_SUBLANES) == 0
    mod4 = mod.reshape(bsz, N_MOD, 1, d)

    def front(s):
        t = jnp.minimum(s, n_total - 1)
        return t // n_tiles, t % n_tiles

    def back(s):
        t = jnp.maximum(s - 2, 0)
        return t // n_tiles, t % n_tiles

    kern = functools.partial(_ffn_kernel, tm=tm, fc=fc, n_tiles=n_tiles)
    return pl.pallas_call(
        kern,
        grid=(n_total + 2,),
        in_specs=[
            pl.BlockSpec((1, tm, d), lambda s: (*front(s), 0)),
            pl.BlockSpec((1, tm, d), lambda s: (*front(s), 0)),
            pl.BlockSpec((1, 1, 1, d), lambda s: (front(s)[0], 3, 0, 0)),
            pl.BlockSpec((1, 1, 1, d), lambda s: (front(s)[0], 4, 0, 0)),
            pl.BlockSpec((1, tm, d), lambda s: (*back(s), 0)),
            pl.BlockSpec((1, tm, d), lambda s: (*back(s), 0)),
            pl.BlockSpec((1, 1, 1, d), lambda s: (back(s)[0], 5, 0, 0)),
            _const_spec((1, d)),
            _const_spec(w_up_bf.shape),
            _const_spec(w_conv.shape),
            _const_spec((1, d_ff)),
            _const_spec(w_down_bf.shape),
            _const_spec((1, d)),
        ],
        out_specs=pl.BlockSpec((1, tm, d), lambda s: (*back(s), 0)),
        out_shape=jax.ShapeDtypeStruct(x.shape, F32),
        scratch_shapes=[
            pltpu.VMEM((FFN_UP_AHEAD + 1, CONV_HALO + tm, fc), F32),
            pltpu.VMEM((CONV_HALO, d_ff), F32),
            pltpu.VMEM((tm, d), BF16),
            pltpu.VMEM((tm, d), BF16),
            pltpu.VMEM((tm, d), F32),
            pltpu.VMEM((tm, d), F32),
        ],
        compiler_params=pltpu.CompilerParams(
            dimension_semantics=("arbitrary",),
            vmem_limit_bytes=V7X_VMEM_LIMIT_BYTES),
        name="ffn",
    )(x, delta, mod4, mod4, x, delta, mod4, g_norm.reshape(1, d), w_up_bf, w_conv,
      b_conv.reshape(1, d_ff), w_down_bf, g_final.reshape(1, d))


def _ssm_params(lam_re, lam_im, log_dt, b_re, b_im, c_re, c_im, d_skip):
    n_groups, n_state = lam_re.shape
    h, t = SSM_GROUP, SSM_BLOCK
    th = t * h
    gps = V7X_LANES // h
    n_slab, n_pair = n_groups // gps, gps // 2
    lam_re, lam_im = lam_re.astype(F32), lam_im.astype(F32)
    dt = jnp.exp(log_dt.astype(F32))[:, None]

    k = jnp.arange(t + 1, dtype=F32)[None, :, None]
    mag = jnp.exp(k * (dt * lam_re)[:, None, :])
    ang = k * (dt * lam_im)[:, None, :]
    p_re, p_im = mag * jnp.cos(ang), mag * jnp.sin(ang)

    a_re, a_im = p_re[:, 1], p_im[:, 1]
    den = lam_re * lam_re + lam_im * lam_im
    q_re = (((a_re - 1.0) * lam_re + a_im * lam_im) / den)[:, None, :]
    q_im = ((a_im * lam_re - (a_re - 1.0) * lam_im) / den)[:, None, :]
    b_re = b_re.astype(F32).transpose(0, 2, 1)
    b_im = b_im.astype(F32).transpose(0, 2, 1)
    bb_re = q_re * b_re - q_im * b_im
    bb_im = q_re * b_im + q_im * b_re
    c_re, c_im = c_re.astype(F32)[:, None], c_im.astype(F32)[:, None]

    pk_re, pk_im = p_re[:, :, None, :], p_im[:, :, None, :]
    cak = jnp.concatenate([c_re * pk_re - c_im * pk_im, -(c_re * pk_im + c_im * pk_re)], axis=-1)
    cat0 = cak[:, :t].reshape(n_groups, th, 2 * n_state)
    cat = cak[:, 1:].reshape(n_groups, th, 2 * n_state)
    r_re, r_im = p_re[:, t - 1::-1][:, :, None, :], p_im[:, t - 1::-1][:, :, None, :]
    ba = jnp.concatenate([r_re * bb_re[:, None] - r_im * bb_im[:, None],
                          r_re * bb_im[:, None] + r_im * bb_re[:, None]], axis=-1)
    ba = ba.reshape(n_groups, th, 2 * n_state)
    bmat = jnp.concatenate([bb_re, bb_im], axis=-1)
    kexp = jnp.einsum('ghk,gnk->ghn', bmat, cat0)
    kexp = kexp + d_skip.astype(F32).reshape(n_groups, h)[:, :, None] * jnp.eye(h, th, dtype=F32)
    toep = jnp.stack([jnp.pad(kexp, ((0, 0), (0, 0), (h * si, 0)))[:, :, :th] for si in range(t)],
                     axis=1).reshape(n_groups, th, th)

    def pair_diag(m):
        m = m.reshape(n_slab, n_pair, 2, th, th)
        z = jnp.zeros_like(m[:, :, 0])
        return jnp.concatenate([jnp.concatenate([m[:, :, 0], z], axis=-1),
                                jnp.concatenate([z, m[:, :, 1]], axis=-1)], axis=-2)

    def pair_state(m):
        m = m.reshape(n_slab, n_pair, 2, th, 2, n_state)
        z = jnp.zeros_like(m[:, :, 0, :, 0])
        top = jnp.concatenate([m[:, :, 0, :, 0], z, m[:, :, 0, :, 1], z], axis=-1)
        bot = jnp.concatenate([z, m[:, :, 1, :, 0], z, m[:, :, 1, :, 1]], axis=-1)
        return jnp.concatenate([top, bot], axis=-2)

    lam_t = jnp.stack([p_re[:, t].reshape(n_slab, n_pair, 2 * n_state),
                       p_im[:, t].reshape(n_slab, n_pair, 2 * n_state)], axis=2)
    lam_t = jnp.broadcast_to(lam_t[:, :, :, None, :], (n_slab, n_pair, 2, BATCH_GROUP, 2 * n_state))
    ca = pair_state(cat).astype(BF16).swapaxes(2, 3)
    return pair_diag(toep).astype(BF16), pair_state(ba).astype(BF16), ca, lam_t


def _layer(x, c, w_ada, b_ada, g_norm_mix, w_in, ssm_lam_re, ssm_lam_im, ssm_log_dt, ssm_b_re,
           ssm_b_im, ssm_c_re, ssm_c_im, ssm_d, w_glu, b_glu, w_pool, b_pool, pool_scale, w_out,
           g_norm_ffn, w_up, w_conv, b_conv, w_down, g_final):
    bsz, seq, d = x.shape
    ssm_w = ssm_d.shape[0]
    tt, tm_ffn, fc = _tile_sizes(seq)
    assert bsz % BATCH_GROUP == 0 and ssm_w % V7X_LANES == 0
    assert seq % tt == 0 and seq % tm_ffn == 0 and tt >= POOL_HALO and tt % SSM_BLOCK == 0

    mod = _adaln(c, w_ada, b_ada)
    toep, ba, ca, lam_t = _ssm_params(ssm_lam_re, ssm_lam_im, ssm_log_dt, ssm_b_re, ssm_b_im,
                                      ssm_c_re, ssm_c_im, ssm_d)
    delta, w_up_bf, w_down_bf = _mixer(
        x, mod, g_norm_mix, w_in.astype(BF16), toep, ba, ca, lam_t, w_glu.astype(BF16), b_glu,
        w_pool.astype(BF16), b_pool, pool_scale, w_out.astype(BF16), w_up, w_down, tt=tt)
    return _ffn(x, delta, mod, g_norm_ffn, w_up_bf, w_conv, b_conv, w_down_bf, g_final,
                tm=tm_ffn, fc=fc)


def kernel(x, c, w_ada, b_ada, g_norm_mix, w_in, ssm_lam_re, ssm_lam_im, ssm_log_dt, ssm_b_re,
           ssm_b_im, ssm_c_re, ssm_c_im, ssm_d, w_glu, b_glu, w_pool, b_pool, pool_scale, w_out,
           g_norm_ffn, w_up, w_conv, b_conv, w_down, g_norm_final):
    assert w_ada.shape[0] == 1, "single-layer block"
    return _layer(x, c, w_ada[0], b_ada[0], g_norm_mix[0], w_in[0], ssm_lam_re[0], ssm_lam_im[0],
                  ssm_log_dt[0], ssm_b_re[0], ssm_b_im[0], ssm_c_re[0], ssm_c_im[0], ssm_d[0],
                  w_glu[0], b_glu[0], w_pool[0], b_pool[0], pool_scale[0], w_out[0],
                  g_norm_ffn[0], w_up[0], w_conv[0], b_conv[0], w_down[0], g_norm_final)
```

```python
import functools

import jax
import jax.numpy as jnp
from jax import lax
from jax.experimental import pallas as pl
from jax.experimental.pallas import tpu as pltpu

V7X_LANES = 128
V7X_SUBLANES = 8
V7X_VMEM_LIMIT_BYTES = 56 * 1024 * 1024

BATCH_GROUP = V7X_SUBLANES

SSM_GROUP = 16
SSM_STATE = 64
SSM_BLOCK = 8
POOL_WINDOWS = (2, 4, 8, 16)
POOL_HALO = 16
POOL_PAD = 8
CONV_WIDTH = 3
CONV_HALO = 8
N_MOD = 6
EPS = 1e-6

F32 = jnp.float32
BF16 = jnp.bfloat16


def _tile_sizes(seq):
    return min(128, seq), min(512, seq), 256


def _const_spec(shape):
    zeros = (0,) * len(shape)
    return pl.BlockSpec(shape, lambda *_: zeros, pipeline_mode=pl.Buffered(1))


def _rms_norm(x, g):
    ms = jnp.mean(x * x, axis=-1, keepdims=True)
    return x * lax.rsqrt(ms + EPS) * g


def _zero_row(width, *arrays):
    acc = None
    for a in arrays:
        bits = lax.bitcast_convert_type(a.astype(F32), jnp.int32)
        rows, cols = bits.shape
        blocks = [bits[r:r + V7X_SUBLANES] for r in range(0, rows, V7X_SUBLANES)]
        bits = functools.reduce(jnp.bitwise_or, blocks)
        tiles = [bits[:, c:c + V7X_LANES] for c in range(0, cols, V7X_LANES)]
        bits = functools.reduce(jnp.bitwise_or, tiles)
        acc = bits if acc is None else acc | bits
    zero = lax.shift_right_logical(lax.shift_right_logical(acc, 16), 16).astype(F32)
    return jnp.concatenate([zero[0:1]] * (width // V7X_LANES), axis=1)


def _adaln_kernel(c_ref, w_ref, b_ref, o_ref):
    c = c_ref[...]
    s = c * jax.nn.sigmoid(c)
    o_ref[...] = jnp.dot(s.astype(BF16), w_ref[...].astype(BF16),
                         preferred_element_type=F32) + b_ref[...]


def _adaln(c, w_ada, b_ada):
    bsz, d = c.shape
    n = w_ada.shape[1]
    return pl.pallas_call(
        _adaln_kernel,
        grid=(n // d,),
        in_specs=[
            pl.BlockSpec((bsz, d), lambda k: (0, 0)),
            pl.BlockSpec((d, d), lambda k: (0, k)),
            pl.BlockSpec((1, d), lambda k: (0, k)),
        ],
        out_specs=pl.BlockSpec((bsz, d), lambda k: (0, k)),
        out_shape=jax.ShapeDtypeStruct((bsz, n), F32),
        name="adaln",
    )(c, w_ada, b_ada.reshape(1, n))


def _gelu_tanh(x, one=1.0):
    return 0.5 * x * (one + jnp.tanh(0.7978845608028654 * (x + 0.044715 * (x * x * x))))


def _granule_transpose(v):
    granule = lax.broadcasted_iota(jnp.int32, v[0].shape, 1) // SSM_GROUP
    v = list(v)
    for d in (4, 2, 1):
        hi = (granule & d) != 0
        nxt = list(v)
        for i in range(len(v)):
            if i & d:
                continue
            lo_v, hi_v = v[i], v[i + d]
            nxt[i] = jnp.where(hi, pltpu.roll(hi_v, SSM_GROUP * d, axis=1), lo_v)
            nxt[i + d] = jnp.where(hi, hi_v, pltpu.roll(lo_v, V7X_LANES - SSM_GROUP * d, axis=1))
        v = nxt
    return v


def _mixer_kernel(x_ref, sh_ref, sc_ref, gt_ref, gmix_ref, win_ref, toep_ref, ba_ref, ca_ref, lam_ref,
                  wglu_ref, bglu_ref, wpool_ref, bpool_ref, pscale_ref, wout_ref, wup_ref, wdown_ref,
                  o_ref, wup_bf_ref, wdown_bf_ref, utm_ref, ytm_ref, yblk_ref, state_ref, halo_ref, lvl_ref, ublk0, ublk1, w0, w1, pbuf0,
                  pbuf1,
                  *, tt, n_tiles, n_total):
    s = pl.program_id(0)
    first_front = jnp.minimum(s, n_total - 1) % n_tiles == 0
    i_back = jnp.maximum(s - 1, 0) % n_tiles
    first_back = i_back == 0
    nb, _, d = x_ref.shape
    n_slab, n_pair = toep_ref.shape[0], toep_ref.shape[1]
    ssm_w = n_slab * V7X_LANES
    pool_w = halo_ref.shape[2]
    pgw = pool_w // len(POOL_WINDOWS)
    nk = tt // SSM_BLOCK
    m = nk * nb
    tile = 2 * V7X_LANES

    wup_bf_ref[...] = wup_ref[...].astype(BF16)
    wdown_bf_ref[...] = wdown_ref[...].astype(BF16)

    @pl.when(s == 0)
    def _():
        ublk1[...] = jnp.zeros_like(ublk1)
        w1[...] = jnp.zeros_like(w1)
        pbuf1[...] = jnp.zeros_like(pbuf1)
        pbuf0[:, 0:POOL_PAD, :] = jnp.zeros((nb, POOL_PAD, pool_w), F32)
        lvl_ref[...] = jnp.zeros_like(lvl_ref)
        halo_ref[...] = jnp.zeros_like(halo_ref)
        state_ref[...] = jnp.zeros_like(state_ref)

    def step(ublk_f, w_f, pbuf_f, ublk_b, w_b, pbuf_b):
        half_d = d // 2

        yblk = []
        for j in range(n_slab):
            for pr in range(n_pair):
                cols = slice(pr * tile, (pr + 1) * tile)
                yb = jnp.dot(ublk_b[j, :, cols], toep_ref[j, pr], preferred_element_type=F32)
                yblk.append(yb + jnp.dot(w_b[j, :, cols].astype(BF16), ca_ref[j, pr],
                                         preferred_element_type=F32))
        h = _rms_norm(x_ref[...], gmix_ref[...])
        h = h * (1.0 + sc_ref[...][:, None, :]) + sh_ref[...][:, None, :]
        hb = h.reshape(nb * tt, d).astype(BF16)
        for j in range(n_slab):
            for pr in range(n_pair):
                yblk_ref[j, :, pr * tile:(pr + 1) * tile] = yblk[j * n_pair + pr]

        z_u = jnp.dot(hb, win_ref[:, :ssm_w], preferred_element_type=F32)
        for j in range(n_slab):
            groups = [yblk_ref[j, :, g * V7X_LANES:(g + 1) * V7X_LANES] for g in range(2 * n_pair)]
            steps = _granule_transpose(groups)
            y4 = jnp.stack([v.reshape(nk, nb, V7X_LANES) for v in steps], axis=1)
            ytm_ref[j] = y4.reshape(tt * nb, V7X_LANES)
        for b in range(nb):
            for j in range(n_slab):
                utm_ref[j, pl.ds(b, tt, stride=nb), :] = (
                    z_u[b * tt:(b + 1) * tt, j * V7X_LANES:(j + 1) * V7X_LANES])

        z_p = jnp.dot(hb, win_ref[:, ssm_w:], preferred_element_type=F32)
        y_nat = jnp.concatenate(
            [jnp.concatenate([ytm_ref[j, pl.ds(b, tt, stride=nb), :] for j in range(n_slab)], axis=1)
             for b in range(nb)], axis=0)
        yg = _gelu_tanh(y_nat).astype(BF16)
        pbuf_f[:, POOL_PAD + POOL_HALO:, :] = z_p.reshape(nb, tt, pool_w)

        lo, hi = POOL_PAD, POOL_PAD + POOL_HALO + tt
        pbuf_b[:, lo:lo + POOL_HALO, :] = jnp.where(first_back, 0.0, halo_ref[...])
        halo_ref[...] = pbuf_b[:, hi - POOL_HALO:hi, :]
        pos = (i_back * tt + 1 + lax.broadcasted_iota(jnp.int32, (1, tt, 1), 1)).astype(F32)

        def pooled(gi):
            win = POOL_WINDOWS[gi]
            cols = slice(gi * pgw, (gi + 1) * pgw)
            tok = pbuf_b[:, lo + POOL_HALO:hi, cols]
            sums = pbuf_b[:, lo:hi, cols] + pbuf_b[:, lo - 1:hi - 1, cols]
            span, level = 2, 0
            while span < win:
                lv = lvl_ref.at[level % 2]
                lv[:, lo:hi, :] = sums
                sums = sums + lv[:, lo - span:hi - span, :]
                span, level = 2 * span, level + 1
            zc = sums[:, POOL_HALO:, :] * (1.0 / jnp.minimum(pos, float(win))) - tok
            return zc.reshape(nb * tt, pgw).astype(BF16)

        n_pg = len(POOL_WINDOWS)
        t_val = jnp.dot(yg, wglu_ref[:, :ssm_w], preferred_element_type=F32)
        zc = [pooled(gi) for gi in range(n_pg // 2)]
        t_gate = jnp.dot(yg, wglu_ref[:, ssm_w:], preferred_element_type=F32)
        zc += [pooled(gi) for gi in range(n_pg // 2, n_pg)]
        y_ssm = ((t_val + bglu_ref[:, :ssm_w])
                 * jax.nn.sigmoid(t_gate + bglu_ref[:, ssm_w:])).astype(BF16)

        for c in range(2):
            cs = slice(c * half_d, (c + 1) * half_d)
            o_ref[:, :, cs] = jnp.dot(y_ssm, wout_ref[:ssm_w, cs],
                                      preferred_element_type=F32).reshape(nb, tt, half_d)

        ubs = []
        for j in range(n_slab):
            u4 = utm_ref[j].reshape(nk, SSM_BLOCK, nb, V7X_LANES)
            steps = [u4[:, q].reshape(m, V7X_LANES) for q in range(SSM_BLOCK)]
            ub = jnp.concatenate(_granule_transpose(steps), axis=1).astype(BF16)
            ublk_f[j] = ub
            ubs.append(ub)
        w_taps = []
        for j in range(n_slab):
            for pr in range(n_pair):
                wp = jnp.dot(ubs[j][:, pr * tile:(pr + 1) * tile], ba_ref[j, pr],
                             preferred_element_type=F32)
                w_f[j, :, pr * tile:(pr + 1) * tile] = wp
                w_taps.append(wp[:, -V7X_LANES:])
        zp = [jnp.dot(zc[gi], wpool_ref[gi], preferred_element_type=F32) for gi in range(n_pg)]
        bpool = bpool_ref[...] + _zero_row(pool_w, *w_taps)
        y_pool = jnp.concatenate(
            [(zp[gi] + bpool[:, gi * pgw:(gi + 1) * pgw]) * pscale_ref[:, gi * pgw:(gi + 1) * pgw]
             for gi in range(n_pg)], axis=-1).astype(BF16)

        chains = [(j, pr, g) for j in range(n_slab) for pr in range(n_pair) for g in range(2)]
        state = [jnp.where(first_front, 0.0, state_ref[j, pr, g]) for j, pr, g in chains]
        for k in range(nk):
            rows = slice(k * nb, (k + 1) * nb)
            for ci, (j, pr, g) in enumerate(chains):
                cols = slice(pr * tile + g * V7X_LANES, pr * tile + (g + 1) * V7X_LANES)
                x_prev = state[ci]
                w_k = w_f[j, rows, cols]
                w_f[j, rows, cols] = x_prev
                state[ci] = (lam_ref[j, pr, g, 0] * x_prev
                             + lam_ref[j, pr, g, 1] * pltpu.roll(x_prev, V7X_LANES // 2, axis=1) + w_k)
        for (j, pr, g), x_last in zip(chains, state):
            state_ref[j, pr, g] = x_last

        for c in range(2):
            cs = slice(c * half_d, (c + 1) * half_d)
            mixed = jnp.dot(y_pool, wout_ref[ssm_w:, cs], preferred_element_type=F32)
            o_ref[:, :, cs] = gt_ref[:, cs][:, None, :] * (o_ref[:, :, cs]
                                                           + mixed.reshape(nb, tt, half_d))

    @pl.when(s % 2 == 0)
    def _():
        step(ublk0, w0, pbuf0, ublk1, w1, pbuf1)

    @pl.when(s % 2 == 1)
    def _():
        step(ublk1, w1, pbuf1, ublk0, w0, pbuf0)


def _cast_block_rows(rows, n_steps):
    tile = 2 * V7X_SUBLANES
    return next(r for r in range(tile, rows + 1, tile) if rows % r == 0 and rows // r <= n_steps)


def _mixer(x, mod, g_norm, w_in_bf, toep, ba, ca, lam_t, w_glu_bf, b_glu, w_pool_bf, b_pool, pool_scale,
           w_out_bf, w_up, w_down, *, tt):
    bsz, seq, d = x.shape
    nb = BATCH_GROUP
    n_slab, n_pair = toep.shape[0], toep.shape[1]
    ssm_w = n_slab * V7X_LANES
    pool_w = w_in_bf.shape[1] - ssm_w
    n_tiles = seq // tt
    n_total = (bsz // nb) * n_tiles
    m = tt // SSM_BLOCK * nb
    width = n_pair * 2 * V7X_LANES

    def front(s):
        t = jnp.minimum(s, n_total - 1)
        return t // n_tiles, t % n_tiles

    def back(s):
        t = jnp.maximum(s - 1, 0)
        return t // n_tiles, t % n_tiles

    up_rows = _cast_block_rows(w_up.shape[0], n_total)
    down_rows = _cast_block_rows(w_down.shape[0], n_total)
    up_spec = pl.BlockSpec((up_rows, w_up.shape[1]),
                           lambda s: (jnp.minimum(s, w_up.shape[0] // up_rows - 1), 0))
    down_spec = pl.BlockSpec((down_rows, w_down.shape[1]),
                             lambda s: (jnp.minimum(s, w_down.shape[0] // down_rows - 1), 0))

    kern = functools.partial(_mixer_kernel, tt=tt, n_tiles=n_tiles, n_total=n_total)
    return pl.pallas_call(
        kern,
        grid=(n_total + 1,),
        in_specs=[
            pl.BlockSpec((nb, tt, d), lambda s: (*front(s), 0)),
            pl.BlockSpec((nb, d), lambda s: (front(s)[0], 0)),
            pl.BlockSpec((nb, d), lambda s: (front(s)[0], 1)),
            pl.BlockSpec((nb, d), lambda s: (back(s)[0], 2)),
            _const_spec((1, d)),
            _const_spec(w_in_bf.shape),
            _const_spec(toep.shape),
            _const_spec(ba.shape),
            _const_spec(ca.shape),
            _const_spec(lam_t.shape),
            _const_spec(w_glu_bf.shape),
            _const_spec((1, 2 * ssm_w)),
            _const_spec(w_pool_bf.shape),
            _const_spec((1, pool_w)),
            _const_spec((1, pool_w)),
            _const_spec(w_out_bf.shape),
            up_spec,
            down_spec,
        ],
        out_specs=[pl.BlockSpec((nb, tt, d), lambda s: (*back(s), 0)), up_spec, down_spec],
        out_shape=[jax.ShapeDtypeStruct(x.shape, F32), jax.ShapeDtypeStruct(w_up.shape, BF16),
                   jax.ShapeDtypeStruct(w_down.shape, BF16)],
        scratch_shapes=[
            pltpu.VMEM((n_slab, tt * nb, V7X_LANES), F32),
            pltpu.VMEM((n_slab, tt * nb, V7X_LANES), F32),
            pltpu.VMEM((n_slab, m, width), F32),
            pltpu.VMEM((n_slab, n_pair, 2, nb, V7X_LANES), F32),
            pltpu.VMEM((nb, POOL_HALO, pool_w), F32),
            pltpu.VMEM((2, nb, POOL_PAD + POOL_HALO + tt, pool_w // len(POOL_WINDOWS)), F32),
            pltpu.VMEM((n_slab, m, width), BF16),
            pltpu.VMEM((n_slab, m, width), BF16),
            pltpu.VMEM((n_slab, m, width), F32),
            pltpu.VMEM((n_slab, m, width), F32),
            pltpu.VMEM((nb, POOL_PAD + POOL_HALO + tt, pool_w), F32),
            pltpu.VMEM((nb, POOL_PAD + POOL_HALO + tt, pool_w), F32),
        ],
        compiler_params=pltpu.CompilerParams(
            dimension_semantics=("arbitrary",),
            vmem_limit_bytes=V7X_VMEM_LIMIT_BYTES),
        name="mixer",
    )(x, mod, mod, mod, g_norm.reshape(1, d), w_in_bf, toep, ba, ca, lam_t, w_glu_bf,
      b_glu.reshape(1, -1), w_pool_bf, b_pool.reshape(1, -1), pool_scale.reshape(1, -1), w_out_bf,
      w_up, w_down)


FFN_ROW_SLICES = 8
FFN_UP_AHEAD = 2


def _ffn_kernel(xf_ref, dlf_ref, sh_ref, sc_ref, xb_ref, dlb_ref, gt_ref, gn_ref, wup_ref, wconv_ref,
                bconv_ref, wdown_ref, gfin_ref, o_ref, gbuf_ref, gprev_ref, h0, h1, acc0, acc1,
                *, tm, fc, n_tiles):
    s = pl.program_id(0)
    first_mid = jnp.maximum(s - 1, 0) % n_tiles == 0
    d_ff = wdown_ref.shape[0]
    n_chunks = d_ff // fc
    n_gbuf = gbuf_ref.shape[0]
    rs = tm // FFN_ROW_SLICES

    @pl.when(s == 0)
    def _():
        h1[...] = jnp.zeros_like(h1)
        acc0[...] = jnp.zeros_like(acc0)
        gprev_ref[...] = jnp.zeros_like(gprev_ref)

    def step(h_f, h_m, acc_m, acc_b):
        def up(c):
            cols = slice(c * fc, (c + 1) * fc)
            gb = gbuf_ref.at[c % n_gbuf]
            h = h_m[...]
            v = jnp.dot(h, wup_ref[:, c * fc:(c + 1) * fc], preferred_element_type=F32)
            g = jnp.dot(h, wup_ref[:, d_ff + c * fc:d_ff + (c + 1) * fc],
                        preferred_element_type=F32)
            gb[0:CONV_HALO, :] = jnp.where(first_mid, 0.0, gprev_ref[:, cols])
            gb[CONV_HALO:, :] = g
            gprev_ref[:, cols] = g[tm - CONV_HALO:, :]
            return v

        pending = {c: up(c) for c in range(FFN_UP_AHEAD)}
        anchor = None
        for c in range(n_chunks):
            cols = slice(c * fc, (c + 1) * fc)
            if c + FFN_UP_AHEAD < n_chunks:
                pending[c + FFN_UP_AHEAD] = up(c + FFN_UP_AHEAD)
            v = pending.pop(c)
            bias = bconv_ref[:, cols]
            if anchor is not None:
                bias = bias + anchor
            gb = gbuf_ref.at[c % n_gbuf]
            gc = bias + wconv_ref[CONV_WIDTH - 1:CONV_WIDTH, cols] * gb[CONV_HALO:, :]
            for k in range(1, CONV_WIDTH):
                gc = gc + (wconv_ref[CONV_WIDTH - 1 - k:CONV_WIDTH - k, cols]
                           * gb[CONV_HALO - k:CONV_HALO - k + tm, :])
            a = (gc * jax.nn.sigmoid(gc) * v).astype(BF16)
            part = jnp.dot(a, wdown_ref[cols, :], preferred_element_type=F32)
            if c == 0:
                acc_m[...] = part
            else:
                acc_m[...] += part
            anchor = None
            if c < FFN_ROW_SLICES:
                rows = slice(c * rs, (c + 1) * rs)
                y = xb_ref[0, rows, :] + dlb_ref[0, rows, :] + gt_ref[0, 0] * acc_b[rows, :]
                y = _rms_norm(y, gfin_ref[...])
                o_ref[0, rows, :] = y
                hf = _rms_norm(xf_ref[0, rows, :] + dlf_ref[0, rows, :], gn_ref[...])
                hf = hf * (1.0 + sc_ref[0, 0]) + sh_ref[0, 0]
                h_f[rows, :] = hf.astype(BF16)
                anchor = _zero_row(fc, y, hf)

    @pl.when(s % 2 == 0)
    def _():
        step(h0, h1, acc1, acc0)

    @pl.when(s % 2 == 1)
    def _():
        step(h1, h0, acc0, acc1)


def _ffn(x, delta, mod, g_norm, w_up_bf, w_conv, b_conv, w_down_bf, g_final, *, tm, fc):
    bsz, seq, d = x.shape
    d_ff = w_down_bf.shape[0]
    n_tiles = seq // tm
    n_total = bsz * n_tiles
    assert d_ff % fc == 0 and d_ff // fc >= FFN_ROW_SLICES and tm % (FFN_ROW_SLICES * V7X_SUBLANES) == 0
    mod4 = mod.reshape(bsz, N_MOD, 1, d)

    def front(s):
        t = jnp.minimum(s, n_total - 1)
        return t // n_tiles, t % n_tiles

    def back(s):
        t = jnp.maximum(s - 2, 0)
        return t // n_tiles, t % n_tiles

    kern = functools.partial(_ffn_kernel, tm=tm, fc=fc, n_tiles=n_tiles)
    return pl.pallas_call(
        kern,
        grid=(n_total + 2,),
        in_specs=[
            pl.BlockSpec((1, tm, d), lambda s: (*front(s), 0)),
            pl.BlockSpec((1, tm, d), lambda s: (*front(s), 0)),
            pl.BlockSpec((1, 1, 1, d), lambda s: (front(s)[0], 3, 0, 0)),
            pl.BlockSpec((1, 1, 1, d), lambda s: (front(s)[0], 4, 0, 0)),
            pl.BlockSpec((1, tm, d), lambda s: (*back(s), 0)),
            pl.BlockSpec((1, tm, d), lambda s: (*back(s), 0)),
            pl.BlockSpec((1, 1, 1, d), lambda s: (back(s)[0], 5, 0, 0)),
            _const_spec((1, d)),
            _const_spec(w_up_bf.shape),
            _const_spec(w_conv.shape),
            _const_spec((1, d_ff)),
            _const_spec(w_down_bf.shape),
            _const_spec((1, d)),
        ],
        out_specs=pl.BlockSpec((1, tm, d), lambda s: (*back(s), 0)),
        out_shape=jax.ShapeDtypeStruct(x.shape, F32),
        scratch_shapes=[
            pltpu.VMEM((FFN_UP_AHEAD + 1, CONV_HALO + tm, fc), F32),
            pltpu.VMEM((CONV_HALO, d_ff), F32),
            pltpu.VMEM((tm, d), BF16),
            pltpu.VMEM((tm, d), BF16),
            pltpu.VMEM((tm, d), F32),
            pltpu.VMEM((tm, d), F32),
        ],
        compiler_params=pltpu.CompilerParams(
            dimension_semantics=("arbitrary",),
            vmem_limit_bytes=V7X_VMEM_LIMIT_BYTES),
        name="ffn",
    )(x, delta, mod4, mod4, x, delta, mod4, g_norm.reshape(1, d), w_up_bf, w_conv,
      b_conv.reshape(1, d_ff), w_down_bf, g_final.reshape(1, d))


def _ssm_params(lam_re, lam_im, log_dt, b_re, b_im, c_re, c_im, d_skip):
    n_groups, n_state = lam_re.shape
    h, t = SSM_GROUP, SSM_BLOCK
    th = t * h
    gps = V7X_LANES // h
    n_slab, n_pair = n_groups // gps, gps // 2
    lam_re, lam_im = lam_re.astype(F32), lam_im.astype(F32)
    dt = jnp.exp(log_dt.astype(F32))[:, None]

    k = jnp.arange(t + 1, dtype=F32)[None, :, None]
    mag = jnp.exp(k * (dt * lam_re)[:, None, :])
    ang = k * (dt * lam_im)[:, None, :]
    p_re, p_im = mag * jnp.cos(ang), mag * jnp.sin(ang)

    a_re, a_im = p_re[:, 1], p_im[:, 1]
    den = lam_re * lam_re + lam_im * lam_im
    q_re = (((a_re - 1.0) * lam_re + a_im * lam_im) / den)[:, None, :]
    q_im = ((a_im * lam_re - (a_re - 1.0) * lam_im) / den)[:, None, :]
    b_re = b_re.astype(F32).transpose(0, 2, 1)
    b_im = b_im.astype(F32).transpose(0, 2, 1)
    bb_re = q_re * b_re - q_im * b_im
    bb_im = q_re * b_im + q_im * b_re
    c_re, c_im = c_re.astype(F32)[:, None], c_im.astype(F32)[:, None]

    pk_re, pk_im = p_re[:, :, None, :], p_im[:, :, None, :]
    cak = jnp.concatenate([c_re * pk_re - c_im * pk_im, -(c_re * pk_im + c_im * pk_re)], axis=-1)
    cat0 = cak[:, :t].reshape(n_groups, th, 2 * n_state)
    cat = cak[:, 1:].reshape(n_groups, th, 2 * n_state)
    r_re, r_im = p_re[:, t - 1::-1][:, :, None, :], p_im[:, t - 1::-1][:, :, None, :]
    ba = jnp.concatenate([r_re * bb_re[:, None] - r_im * bb_im[:, None],
                          r_re * bb_im[:, None] + r_im * bb_re[:, None]], axis=-1)
    ba = ba.reshape(n_groups, th, 2 * n_state)
    bmat = jnp.concatenate([bb_re, bb_im], axis=-1)
    kexp = jnp.einsum('ghk,gnk->ghn', bmat, cat0)
    kexp = kexp + d_skip.astype(F32).reshape(n_groups, h)[:, :, None] * jnp.eye(h, th, dtype=F32)
    toep = jnp.stack([jnp.pad(kexp, ((0, 0), (0, 0), (h * si, 0)))[:, :, :th] for si in range(t)],
                     axis=1).reshape(n_groups, th, th)

    def pair_diag(m):
        m = m.reshape(n_slab, n_pair, 2, th, th)
        z = jnp.zeros_like(m[:, :, 0])
        return jnp.concatenate([jnp.concatenate([m[:, :, 0], z], axis=-1),
                                jnp.concatenate([z, m[:, :, 1]], axis=-1)], axis=-2)

    lam_t = jnp.stack([jnp.concatenate([p_re[:, t], p_re[:, t]], axis=-1),
                       jnp.concatenate([-p_im[:, t], p_im[:, t]], axis=-1)], axis=1)
    lam_t = jnp.broadcast_to(lam_t.reshape(n_slab, n_pair, 2, 2, 1, 2 * n_state),
                             (n_slab, n_pair, 2, 2, BATCH_GROUP, 2 * n_state))
    ca = pair_diag(cat).astype(BF16).swapaxes(2, 3)
    return pair_diag(toep).astype(BF16), pair_diag(ba).astype(BF16), ca, lam_t


def _layer(x, c, w_ada, b_ada, g_norm_mix, w_in, ssm_lam_re, ssm_lam_im, ssm_log_dt, ssm_b_re,
           ssm_b_im, ssm_c_re, ssm_c_im, ssm_d, w_glu, b_glu, w_pool, b_pool, pool_scale, w_out,
           g_norm_ffn, w_up, w_conv, b_conv, w_down, g_final):
    bsz, seq, d = x.shape
    ssm_w = ssm_d.shape[0]
    tt, tm_ffn, fc = _tile_sizes(seq)
    assert bsz % BATCH_GROUP == 0 and ssm_w % V7X_LANES == 0
    assert seq % tt == 0 and seq % tm_ffn == 0 and tt >= POOL_HALO and tt % SSM_BLOCK == 0

    mod = _adaln(c, w_ada, b_ada)
    toep, ba, ca, lam_t = _ssm_params(ssm_lam_re, ssm_lam_im, ssm_log_dt, ssm_b_re, ssm_b_im,
                                      ssm_c_re, ssm_c_im, ssm_d)
    delta, w_up_bf, w_down_bf = _mixer(
        x, mod, g_norm_mix, w_in.astype(BF16), toep, ba, ca, lam_t, w_glu.astype(BF16), b_glu,
        w_pool.astype(BF16), b_pool, pool_scale, w_out.astype(BF16), w_up, w_down, tt=tt)
    return _ffn(x, delta, mod, g_norm_ffn, w_up_bf, w_conv, b_conv, w_down_bf, g_final,
                tm=tm_ffn, fc=fc)


def kernel(x, c, w_ada, b_ada, g_norm_mix, w_in, ssm_lam_re, ssm_lam_im, ssm_log_dt, ssm_b_re,
           ssm_b_im, ssm_c_re, ssm_c_im, ssm_d, w_glu, b_glu, w_pool, b_pool, pool_scale, w_out,
           g_norm_ffn, w_up, w_conv, b_conv, w_down, g_norm_final):
    assert w_ada.shape[0] == 1, "single-layer block"
    return _layer(x, c, w_ada[0], b_ada[0], g_norm_mix[0], w_in[0], ssm_lam_re[0], ssm_lam_im[0],
                  ssm_log_dt[0], ssm_b_re[0], ssm_b_im[0], ssm_c_re[0], ssm_c_im[0], ssm_d[0],
                  w_glu[0], b_glu[0], w_pool[0], b_pool[0], pool_scale[0], w_out[0],
                  g_norm_ffn[0], w_up[0], w_conv[0], b_conv[0], w_down[0], g_norm_final)
```

```python
import functools

import jax
import jax.numpy as jnp
from jax import lax
from jax.experimental import pallas as pl
from jax.experimental.pallas import tpu as pltpu

V7X_LANES = 128
V7X_SUBLANES = 8
V7X_VMEM_LIMIT_BYTES = 56 * 1024 * 1024

BATCH_GROUP = V7X_SUBLANES

SSM_GROUP = 16
SSM_STATE = 64
SSM_BLOCK = 8
POOL_WINDOWS = (2, 4, 8, 16)
POOL_HALO = 16
POOL_PAD = 8
CONV_WIDTH = 3
CONV_HALO = 8
N_MOD = 6
EPS = 1e-6

F32 = jnp.float32
BF16 = jnp.bfloat16


def _tile_sizes(seq):
    return min(128, seq), min(512, seq), 256


def _const_spec(shape):
    zeros = (0,) * len(shape)
    return pl.BlockSpec(shape, lambda *_: zeros, pipeline_mode=pl.Buffered(1))


def _rms_norm(x, g):
    ms = jnp.mean(x * x, axis=-1, keepdims=True)
    return x * lax.rsqrt(ms + EPS) * g


def _zero_row(width, *arrays):
    acc = None
    for a in arrays:
        bits = lax.bitcast_convert_type(a.astype(F32), jnp.int32)
        rows, cols = bits.shape
        blocks = [bits[r:r + V7X_SUBLANES] for r in range(0, rows, V7X_SUBLANES)]
        bits = functools.reduce(jnp.bitwise_or, blocks)
        tiles = [bits[:, c:c + V7X_LANES] for c in range(0, cols, V7X_LANES)]
        bits = functools.reduce(jnp.bitwise_or, tiles)
        acc = bits if acc is None else acc | bits
    zero = lax.shift_right_logical(lax.shift_right_logical(acc, 16), 16).astype(F32)
    return jnp.concatenate([zero[0:1]] * (width // V7X_LANES), axis=1)


def _adaln_kernel(c_ref, w_ref, b_ref, o_ref):
    c = c_ref[...]
    s = c * jax.nn.sigmoid(c)
    o_ref[...] = jnp.dot(s.astype(BF16), w_ref[...].astype(BF16),
                         preferred_element_type=F32) + b_ref[...]


def _adaln(c, w_ada, b_ada):
    bsz, d = c.shape
    n = w_ada.shape[1]
    return pl.pallas_call(
        _adaln_kernel,
        grid=(n // d,),
        in_specs=[
            pl.BlockSpec((bsz, d), lambda k: (0, 0)),
            pl.BlockSpec((d, d), lambda k: (0, k)),
            pl.BlockSpec((1, d), lambda k: (0, k)),
        ],
        out_specs=pl.BlockSpec((bsz, d), lambda k: (0, k)),
        out_shape=jax.ShapeDtypeStruct((bsz, n), F32),
        name="adaln",
    )(c, w_ada, b_ada.reshape(1, n))


def _gelu_tanh(x):
    return 0.5 * x * (1.0 + jnp.tanh(0.7978845608028654 * (x + 0.044715 * (x * x * x))))


def _granule_transpose(v):
    granule = lax.broadcasted_iota(jnp.int32, v[0].shape, 1) // SSM_GROUP
    v = list(v)
    for d in (4, 2, 1):
        hi = (granule & d) != 0
        nxt = list(v)
        for i in range(len(v)):
            if i & d:
                continue
            lo_v, hi_v = v[i], v[i + d]
            nxt[i] = jnp.where(hi, pltpu.roll(hi_v, SSM_GROUP * d, axis=1), lo_v)
            nxt[i + d] = jnp.where(hi, hi_v, pltpu.roll(lo_v, V7X_LANES - SSM_GROUP * d, axis=1))
        v = nxt
    return v


def _mixer_kernel(x_ref, sh_ref, sc_ref, gt_ref, gmix_ref, win_ref, toep_ref, ba_ref, ca_ref, lam_ref,
                  wglu_ref, bglu_ref, wpool_ref, bpool_ref, pscale_ref, wout_ref, wup_ref, wdown_ref,
                  o_ref, wup_bf_ref, wdown_bf_ref, utm_ref, ytm_ref, yblk_ref, state_ref, halo_ref, lvl_ref, ublk0, ublk1, w0, w1, pbuf0,
                  pbuf1,
                  *, tt, n_tiles, n_total):
    s = pl.program_id(0)
    first_front = jnp.minimum(s, n_total - 1) % n_tiles == 0
    i_back = jnp.maximum(s - 1, 0) % n_tiles
    first_back = i_back == 0
    nb, _, d = x_ref.shape
    n_slab, n_pair = toep_ref.shape[0], toep_ref.shape[1]
    ssm_w = n_slab * V7X_LANES
    pool_w = halo_ref.shape[2]
    pgw = pool_w // len(POOL_WINDOWS)
    nk = tt // SSM_BLOCK
    m = nk * nb
    tile = 2 * V7X_LANES

    wup_bf_ref[...] = wup_ref[...].astype(BF16)
    wdown_bf_ref[...] = wdown_ref[...].astype(BF16)

    @pl.when(s == 0)
    def _():
        ublk1[...] = jnp.zeros_like(ublk1)
        w1[...] = jnp.zeros_like(w1)
        pbuf1[...] = jnp.zeros_like(pbuf1)
        pbuf0[:, 0:POOL_PAD, :] = jnp.zeros((nb, POOL_PAD, pool_w), F32)
        lvl_ref[...] = jnp.zeros_like(lvl_ref)
        halo_ref[...] = jnp.zeros_like(halo_ref)
        state_ref[...] = jnp.zeros_like(state_ref)

    def step(ublk_f, w_f, pbuf_f, ublk_b, w_b, pbuf_b):
        half_d = d // 2

        yblk = []
        for j in range(n_slab):
            for pr in range(n_pair):
                cols = slice(pr * tile, (pr + 1) * tile)
                yb = jnp.dot(ublk_b[j, :, cols], toep_ref[j, pr], preferred_element_type=F32)
                yblk.append(yb + jnp.dot(w_b[j, :, cols].astype(BF16), ca_ref[j, pr],
                                         preferred_element_type=F32))
        h = _rms_norm(x_ref[...], gmix_ref[...])
        h = h * (1.0 + sc_ref[...][:, None, :]) + sh_ref[...][:, None, :]
        hb = h.reshape(nb * tt, d).astype(BF16)
        for j in range(n_slab):
            for pr in range(n_pair):
                yblk_ref[j, :, pr * tile:(pr + 1) * tile] = yblk[j * n_pair + pr]

        z_u = jnp.dot(hb, win_ref[:, :ssm_w], preferred_element_type=F32)
        for j in range(n_slab):
            groups = [yblk_ref[j, :, g * V7X_LANES:(g + 1) * V7X_LANES] for g in range(2 * n_pair)]
            steps = _granule_transpose(groups)
            y4 = jnp.stack([v.reshape(nk, nb, V7X_LANES) for v in steps], axis=1)
            ytm_ref[j] = y4.reshape(tt * nb, V7X_LANES)
        for b in range(nb):
            for j in range(n_slab):
                utm_ref[j, pl.ds(b, tt, stride=nb), :] = (
                    z_u[b * tt:(b + 1) * tt, j * V7X_LANES:(j + 1) * V7X_LANES])

        z_p = jnp.dot(hb, win_ref[:, ssm_w:], preferred_element_type=F32)
        y_nat = jnp.concatenate(
            [jnp.concatenate([ytm_ref[j, pl.ds(b, tt, stride=nb), :] for j in range(n_slab)], axis=1)
             for b in range(nb)], axis=0)
        yg = _gelu_tanh(y_nat).astype(BF16)
        pbuf_f[:, POOL_PAD + POOL_HALO:, :] = z_p.reshape(nb, tt, pool_w)

        lo, hi = POOL_PAD, POOL_PAD + POOL_HALO + tt
        pbuf_b[:, lo:lo + POOL_HALO, :] = jnp.where(first_back, 0.0, halo_ref[...])
        halo_ref[...] = pbuf_b[:, hi - POOL_HALO:hi, :]
        pos = (i_back * tt + 1 + lax.broadcasted_iota(jnp.int32, (1, tt, 1), 1)).astype(F32)

        def pooled(gi):
            win = POOL_WINDOWS[gi]
            cols = slice(gi * pgw, (gi + 1) * pgw)
            tok = pbuf_b[:, lo + POOL_HALO:hi, cols]
            sums = pbuf_b[:, lo:hi, cols] + pbuf_b[:, lo - 1:hi - 1, cols]
            span, level = 2, 0
            while span < win:
                lv = lvl_ref.at[level % 2]
                lv[:, lo:hi, :] = sums
                sums = sums + lv[:, lo - span:hi - span, :]
                span, level = 2 * span, level + 1
            zc = sums[:, POOL_HALO:, :] * (1.0 / jnp.minimum(pos, float(win))) - tok
            return zc.reshape(nb * tt, pgw).astype(BF16)

        n_pg = len(POOL_WINDOWS)
        t_val = jnp.dot(yg, wglu_ref[:, :ssm_w], preferred_element_type=F32)
        zc = [pooled(gi) for gi in range(n_pg // 2)]
        t_gate = jnp.dot(yg, wglu_ref[:, ssm_w:], preferred_element_type=F32)
        zc += [pooled(gi) for gi in range(n_pg // 2, n_pg)]
        y_ssm = ((t_val + bglu_ref[:, :ssm_w])
                 * jax.nn.sigmoid(t_gate + bglu_ref[:, ssm_w:])).astype(BF16)

        for c in range(2):
            cs = slice(c * half_d, (c + 1) * half_d)
            o_ref[:, :, cs] = jnp.dot(y_ssm, wout_ref[:ssm_w, cs],
                                      preferred_element_type=F32).reshape(nb, tt, half_d)

        ubs = []
        for j in range(n_slab):
            u4 = utm_ref[j].reshape(nk, SSM_BLOCK, nb, V7X_LANES)
            steps = [u4[:, q].reshape(m, V7X_LANES) for q in range(SSM_BLOCK)]
            ub = jnp.concatenate(_granule_transpose(steps), axis=1).astype(BF16)
            ublk_f[j] = ub
            ubs.append(ub)
        for j in range(n_slab):
            for pr in range(n_pair):
                cols = slice(pr * tile, (pr + 1) * tile)
                w_f[j, :, cols] = jnp.dot(ubs[j][:, cols], ba_ref[j, pr], preferred_element_type=F32)
        y_pool = []
        for gi in range(n_pg):
            cols = slice(gi * pgw, (gi + 1) * pgw)
            zp = jnp.dot(zc[gi], wpool_ref[gi], preferred_element_type=F32)
            y_pool.append((zp + bpool_ref[:, cols]) * pscale_ref[:, cols])
        y_pool = jnp.concatenate(y_pool, axis=-1).astype(BF16)

        chains = [(j, pr, g) for j in range(n_slab) for pr in range(n_pair) for g in range(2)]
        state = [jnp.where(first_front, 0.0, state_ref[j, pr, g]) for j, pr, g in chains]
        for k in range(nk):
            rows = slice(k * nb, (k + 1) * nb)
            for ci, (j, pr, g) in enumerate(chains):
                cols = slice(pr * tile + g * V7X_LANES, pr * tile + (g + 1) * V7X_LANES)
                x_prev = state[ci]
                w_k = w_f[j, rows, cols]
                w_f[j, rows, cols] = x_prev
                state[ci] = (lam_ref[j, pr, g, 0] * x_prev
                             + lam_ref[j, pr, g, 1] * pltpu.roll(x_prev, V7X_LANES // 2, axis=1) + w_k)
        for (j, pr, g), x_last in zip(chains, state):
            state_ref[j, pr, g] = x_last

        for c in range(2):
            cs = slice(c * half_d, (c + 1) * half_d)
            mixed = jnp.dot(y_pool, wout_ref[ssm_w:, cs], preferred_element_type=F32)
            o_ref[:, :, cs] = gt_ref[:, cs][:, None, :] * (o_ref[:, :, cs]
                                                           + mixed.reshape(nb, tt, half_d))

    @pl.when(s % 2 == 0)
    def _():
        step(ublk0, w0, pbuf0, ublk1, w1, pbuf1)

    @pl.when(s % 2 == 1)
    def _():
        step(ublk1, w1, pbuf1, ublk0, w0, pbuf0)


def _cast_block_rows(rows, n_steps):
    tile = 2 * V7X_SUBLANES
    return next(r for r in range(tile, rows + 1, tile) if rows % r == 0 and rows // r <= n_steps)


def _mixer(x, mod, g_norm, w_in_bf, toep, ba, ca, lam_t, w_glu_bf, b_glu, w_pool_bf, b_pool, pool_scale,
           w_out_bf, w_up, w_down, *, tt):
    bsz, seq, d = x.shape
    nb = BATCH_GROUP
    n_slab, n_pair = toep.shape[0], toep.shape[1]
    ssm_w = n_slab * V7X_LANES
    pool_w = w_in_bf.shape[1] - ssm_w
    n_tiles = seq // tt
    n_total = (bsz // nb) * n_tiles
    m = tt // SSM_BLOCK * nb
    width = n_pair * 2 * V7X_LANES

    def front(s):
        t = jnp.minimum(s, n_total - 1)
        return t // n_tiles, t % n_tiles

    def back(s):
        t = jnp.maximum(s - 1, 0)
        return t // n_tiles, t % n_tiles

    up_rows = _cast_block_rows(w_up.shape[0], n_total)
    down_rows = _cast_block_rows(w_down.shape[0], n_total)
    up_spec = pl.BlockSpec((up_rows, w_up.shape[1]),
                           lambda s: (jnp.minimum(s, w_up.shape[0] // up_rows - 1), 0))
    down_spec = pl.BlockSpec((down_rows, w_down.shape[1]),
                             lambda s: (jnp.minimum(s, w_down.shape[0] // down_rows - 1), 0))

    kern = functools.partial(_mixer_kernel, tt=tt, n_tiles=n_tiles, n_total=n_total)
    return pl.pallas_call(
        kern,
        grid=(n_total + 1,),
        in_specs=[
            pl.BlockSpec((nb, tt, d), lambda s: (*front(s), 0)),
            pl.BlockSpec((nb, d), lambda s: (front(s)[0], 0)),
            pl.BlockSpec((nb, d), lambda s: (front(s)[0], 1)),
            pl.BlockSpec((nb, d), lambda s: (back(s)[0], 2)),
            _const_spec((1, d)),
            _const_spec(w_in_bf.shape),
            _const_spec(toep.shape),
            _const_spec(ba.shape),
            _const_spec(ca.shape),
            _const_spec(lam_t.shape),
            _const_spec(w_glu_bf.shape),
            _const_spec((1, 2 * ssm_w)),
            _const_spec(w_pool_bf.shape),
            _const_spec((1, pool_w)),
            _const_spec((1, pool_w)),
            _const_spec(w_out_bf.shape),
            up_spec,
            down_spec,
        ],
        out_specs=[pl.BlockSpec((nb, tt, d), lambda s: (*back(s), 0)), up_spec, down_spec],
        out_shape=[jax.ShapeDtypeStruct(x.shape, F32), jax.ShapeDtypeStruct(w_up.shape, BF16),
                   jax.ShapeDtypeStruct(w_down.shape, BF16)],
        scratch_shapes=[
            pltpu.VMEM((n_slab, tt * nb, V7X_LANES), F32),
            pltpu.VMEM((n_slab, tt * nb, V7X_LANES), F32),
            pltpu.VMEM((n_slab, m, width), F32),
            pltpu.VMEM((n_slab, n_pair, 2, nb, V7X_LANES), F32),
            pltpu.VMEM((nb, POOL_HALO, pool_w), F32),
            pltpu.VMEM((2, nb, POOL_PAD + POOL_HALO + tt, pool_w // len(POOL_WINDOWS)), F32),
            pltpu.VMEM((n_slab, m, width), BF16),
            pltpu.VMEM((n_slab, m, width), BF16),
            pltpu.VMEM((n_slab, m, width), F32),
            pltpu.VMEM((n_slab, m, width), F32),
            pltpu.VMEM((nb, POOL_PAD + POOL_HALO + tt, pool_w), F32),
            pltpu.VMEM((nb, POOL_PAD + POOL_HALO + tt, pool_w), F32),
        ],
        compiler_params=pltpu.CompilerParams(
            dimension_semantics=("arbitrary",),
            vmem_limit_bytes=V7X_VMEM_LIMIT_BYTES),
        name="mixer",
    )(x, mod, mod, mod, g_norm.reshape(1, d), w_in_bf, toep, ba, ca, lam_t, w_glu_bf,
      b_glu.reshape(1, -1), w_pool_bf, b_pool.reshape(1, -1), pool_scale.reshape(1, -1), w_out_bf,
      w_up, w_down)


FFN_ROW_SLICES = 8
FFN_UP_AHEAD = 2


def _ffn_kernel(xf_ref, dlf_ref, sh_ref, sc_ref, xb_ref, dlb_ref, gt_ref, gn_ref, wup_ref, wconv_ref,
                bconv_ref, wdown_ref, gfin_ref, o_ref, gbuf_ref, gprev_ref, h0, h1, acc0, acc1,
                *, tm, fc, n_tiles):
    s = pl.program_id(0)
    first_mid = jnp.maximum(s - 1, 0) % n_tiles == 0
    d_ff = wdown_ref.shape[0]
    n_chunks = d_ff // fc
    n_gbuf = gbuf_ref.shape[0]
    rs = tm // FFN_ROW_SLICES

    @pl.when(s == 0)
    def _():
        h1[...] = jnp.zeros_like(h1)
        acc0[...] = jnp.zeros_like(acc0)
        gprev_ref[...] = jnp.zeros_like(gprev_ref)

    def step(h_f, h_m, acc_m, acc_b):
        def up(c):
            cols = slice(c * fc, (c + 1) * fc)
            gb = gbuf_ref.at[c % n_gbuf]
            h = h_m[...]
            v = jnp.dot(h, wup_ref[:, c * fc:(c + 1) * fc], preferred_element_type=F32)
            g = jnp.dot(h, wup_ref[:, d_ff + c * fc:d_ff + (c + 1) * fc],
                        preferred_element_type=F32)
            gb[0:CONV_HALO, :] = jnp.where(first_mid, 0.0, gprev_ref[:, cols])
            gb[CONV_HALO:, :] = g
            gprev_ref[:, cols] = g[tm - CONV_HALO:, :]
            return v

        pending = {c: up(c) for c in range(FFN_UP_AHEAD)}
        anchor = None
        for c in range(n_chunks):
            cols = slice(c * fc, (c + 1) * fc)
            if c + FFN_UP_AHEAD < n_chunks:
                pending[c + FFN_UP_AHEAD] = up(c + FFN_UP_AHEAD)
            v = pending.pop(c)
            bias = bconv_ref[:, cols]
            if anchor is not None:
                bias = bias + anchor
            gb = gbuf_ref.at[c % n_gbuf]
            gc = bias + wconv_ref[CONV_WIDTH - 1:CONV_WIDTH, cols] * gb[CONV_HALO:, :]
            for k in range(1, CONV_WIDTH):
                gc = gc + (wconv_ref[CONV_WIDTH - 1 - k:CONV_WIDTH - k, cols]
                           * gb[CONV_HALO - k:CONV_HALO - k + tm, :])
            a = (gc * jax.nn.sigmoid(gc) * v).astype(BF16)
            part = jnp.dot(a, wdown_ref[cols, :], preferred_element_type=F32)
            if c == 0:
                acc_m[...] = part
            else:
                acc_m[...] += part
            anchor = None
            if c < FFN_ROW_SLICES:
                rows = slice(c * rs, (c + 1) * rs)
                y = xb_ref[0, rows, :] + dlb_ref[0, rows, :] + gt_ref[0, 0] * acc_b[rows, :]
                y = _rms_norm(y, gfin_ref[...])
                o_ref[0, rows, :] = y
                hf = _rms_norm(xf_ref[0, rows, :] + dlf_ref[0, rows, :], gn_ref[...])
                hf = hf * (1.0 + sc_ref[0, 0]) + sh_ref[0, 0]
                h_f[rows, :] = hf.astype(BF16)
                anchor = _zero_row(fc, y, hf)

    @pl.when(s % 2 == 0)
    def _():
        step(h0, h1, acc1, acc0)

    @pl.when(s % 2 == 1)
    def _():
        step(h1, h0, acc0, acc1)


def _ffn(x, delta, mod, g_norm, w_up_bf, w_conv, b_conv, w_down_bf, g_final, *, tm, fc):
    bsz, seq, d = x.shape
    d_ff = w_down_bf.shape[0]
    n_tiles = seq // tm
    n_total = bsz * n_tiles
    assert d_ff % fc == 0 and d_ff // fc >= FFN_ROW_SLICES and tm % (FFN_ROW_SLICES * V7X_SUBLANES) == 0
    mod4 = mod.reshape(bsz, N_MOD, 1, d)

    def front(s):
        t = jnp.minimum(s, n_total - 1)
        return t // n_tiles, t % n_tiles

    def back(s):
        t = jnp.maximum(s - 2, 0)
        return t // n_tiles, t % n_tiles

    kern = functools.partial(_ffn_kernel, tm=tm, fc=fc, n_tiles=n_tiles)
    return pl.pallas_call(
        kern,
        grid=(n_total + 2,),
        in_specs=[
            pl.BlockSpec((1, tm, d), lambda s: (*front(s), 0)),
            pl.BlockSpec((1, tm, d), lambda s: (*front(s), 0)),
            pl.BlockSpec((1, 1, 1, d), lambda s: (front(s)[0], 3, 0, 0)),
            pl.BlockSpec((1, 1, 1, d), lambda s: (front(s)[0], 4, 0, 0)),
            pl.BlockSpec((1, tm, d), lambda s: (*back(s), 0)),
            pl.BlockSpec((1, tm, d), lambda s: (*back(s), 0)),
            pl.BlockSpec((1, 1, 1, d), lambda s: (back(s)[0], 5, 0, 0)),
            _const_spec((1, d)),
            _const_spec(w_up_bf.shape),
            _const_spec(w_conv.shape),
            _const_spec((1, d_ff)),
            _const_spec(w_down_bf.shape),
            _const_spec((1, d)),
        ],
        out_specs=pl.BlockSpec((1, tm, d), lambda s: (*back(s), 0)),
        out_shape=jax.ShapeDtypeStruct(x.shape, F32),
        scratch_shapes=[
            pltpu.VMEM((FFN_UP_AHEAD + 1, CONV_HALO + tm, fc), F32),
            pltpu.VMEM((CONV_HALO, d_ff), F32),
            pltpu.VMEM((tm, d), BF16),
            pltpu.VMEM((tm, d), BF16),
            pltpu.VMEM((tm, d), F32),
            pltpu.VMEM((tm, d), F32),
        ],
        compiler_params=pltpu.CompilerParams(
            dimension_semantics=("arbitrary",),
            vmem_limit_bytes=V7X_VMEM_LIMIT_BYTES),
        name="ffn",
    )(x, delta, mod4, mod4, x, delta, mod4, g_norm.reshape(1, d), w_up_bf, w_conv,
      b_conv.reshape(1, d_ff), w_down_bf, g_final.reshape(1, d))


def _ssm_params(lam_re, lam_im, log_dt, b_re, b_im, c_re, c_im, d_skip):
    n_groups, n_state = lam_re.shape
    h, t = SSM_GROUP, SSM_BLOCK
    th = t * h
    gps = V7X_LANES // h
    n_slab, n_pair = n_groups // gps, gps // 2
    lam_re, lam_im = lam_re.astype(F32), lam_im.astype(F32)
    dt = jnp.exp(log_dt.astype(F32))[:, None]

    k = jnp.arange(t + 1, dtype=F32)[None, :, None]
    mag = jnp.exp(k * (dt * lam_re)[:, None, :])
    ang = k * (dt * lam_im)[:, None, :]
    p_re, p_im = mag * jnp.cos(ang), mag * jnp.sin(ang)

    a_re, a_im = p_re[:, 1], p_im[:, 1]
    den = lam_re * lam_re + lam_im * lam_im
    q_re = (((a_re - 1.0) * lam_re + a_im * lam_im) / den)[:, None, :]
    q_im = ((a_im * lam_re - (a_re - 1.0) * lam_im) / den)[:, None, :]
    b_re = b_re.astype(F32).transpose(0, 2, 1)
    b_im = b_im.astype(F32).transpose(0, 2, 1)
    bb_re = q_re * b_re - q_im * b_im
    bb_im = q_re * b_im + q_im * b_re
    c_re, c_im = c_re.astype(F32)[:, None], c_im.astype(F32)[:, None]

    pk_re, pk_im = p_re[:, :, None, :], p_im[:, :, None, :]
    cak = jnp.concatenate([c_re * pk_re - c_im * pk_im, -(c_re * pk_im + c_im * pk_re)], axis=-1)
    cat0 = cak[:, :t].reshape(n_groups, th, 2 * n_state)
    cat = cak[:, 1:].reshape(n_groups, th, 2 * n_state)
    r_re, r_im = p_re[:, t - 1::-1][:, :, None, :], p_im[:, t - 1::-1][:, :, None, :]
    ba = jnp.concatenate([r_re * bb_re[:, None] - r_im * bb_im[:, None],
                          r_re * bb_im[:, None] + r_im * bb_re[:, None]], axis=-1)
    ba = ba.reshape(n_groups, th, 2 * n_state)
    bmat = jnp.concatenate([bb_re, bb_im], axis=-1)
    kexp = jnp.einsum('ghk,gnk->ghn', bmat, cat0)
    kexp = kexp + d_skip.astype(F32).reshape(n_groups, h)[:, :, None] * jnp.eye(h, th, dtype=F32)
    toep = jnp.stack([jnp.pad(kexp, ((0, 0), (0, 0), (h * si, 0)))[:, :, :th] for si in range(t)],
                     axis=1).reshape(n_groups, th, th)

    def pair_diag(m):
        m = m.reshape(n_slab, n_pair, 2, th, th)
        z = jnp.zeros_like(m[:, :, 0])
        return jnp.concatenate([jnp.concatenate([m[:, :, 0], z], axis=-1),
                                jnp.concatenate([z, m[:, :, 1]], axis=-1)], axis=-2)

    lam_t = jnp.stack([jnp.concatenate([p_re[:, t], p_re[:, t]], axis=-1),
                       jnp.concatenate([-p_im[:, t], p_im[:, t]], axis=-1)], axis=1)
    lam_t = jnp.broadcast_to(lam_t.reshape(n_slab, n_pair, 2, 2, 1, 2 * n_state),
                             (n_slab, n_pair, 2, 2, BATCH_GROUP, 2 * n_state))
    ca = pair_diag(cat).astype(BF16).swapaxes(2, 3)
    return pair_diag(toep).astype(BF16), pair_diag(ba).astype(BF16), ca, lam_t


def _layer(x, c, w_ada, b_ada, g_norm_mix, w_in, ssm_lam_re, ssm_lam_im, ssm_log_dt, ssm_b_re,
           ssm_b_im, ssm_c_re, ssm_c_im, ssm_d, w_glu, b_glu, w_pool, b_pool, pool_scale, w_out,
           g_norm_ffn, w_up, w_conv, b_conv, w_down, g_final):
    bsz, seq, d = x.shape
    ssm_w = ssm_d.shape[0]
    tt, tm_ffn, fc = _tile_sizes(seq)
    assert bsz % BATCH_GROUP == 0 and ssm_w % V7X_LANES == 0
    assert seq % tt == 0 and seq % tm_ffn == 0 and tt >= POOL_HALO and tt % SSM_BLOCK == 0

    mod = _adaln(c, w_ada, b_ada)
    toep, ba, ca, lam_t = _ssm_params(ssm_lam_re, ssm_lam_im, ssm_log_dt, ssm_b_re, ssm_b_im,
                                      ssm_c_re, ssm_c_im, ssm_d)
    delta, w_up_bf, w_down_bf = _mixer(
        x, mod, g_norm_mix, w_in.astype(BF16), toep, ba, ca, lam_t, w_glu.astype(BF16), b_glu,
        w_pool.astype(BF16), b_pool, pool_scale, w_out.astype(BF16), w_up, w_down, tt=tt)
    return _ffn(x, delta, mod, g_norm_ffn, w_up_bf, w_conv, b_conv, w_down_bf, g_final,
                tm=tm_ffn, fc=fc)


def kernel(x, c, w_ada, b_ada, g_norm_mix, w_in, ssm_lam_re, ssm_lam_im, ssm_log_dt, ssm_b_re,
           ssm_b_im, ssm_c_re, ssm_c_im, ssm_d, w_glu, b_glu, w_pool, b_pool, pool_scale, w_out,
           g_norm_ffn, w_up, w_conv, b_conv, w_down, g_norm_final):
    assert w_ada.shape[0] == 1, "single-layer block"
    return _layer(x, c, w_ada[0], b_ada[0], g_norm_mix[0], w_in[0], ssm_lam_re[0], ssm_lam_im[0],
                  ssm_log_dt[0], ssm_b_re[0], ssm_b_im[0], ssm_c_re[0], ssm_c_im[0], ssm_d[0],
                  w_glu[0], b_glu[0], w_pool[0], b_pool[0], pool_scale[0], w_out[0],
                  g_norm_ffn[0], w_up[0], w_conv[0], b_conv[0], w_down[0], g_norm_final)
```

```python
import functools

import jax
import jax.numpy as jnp
from jax import lax
from jax.experimental import pallas as pl
from jax.experimental.pallas import tpu as pltpu

V7X_LANES = 128
V7X_SUBLANES = 8
V7X_VMEM_LIMIT_BYTES = 56 * 1024 * 1024

BATCH_GROUP = V7X_SUBLANES

SSM_GROUP = 16
SSM_STATE = 64
SSM_BLOCK = 8
POOL_WINDOWS = (2, 4, 8, 16)
POOL_HALO = 16
POOL_PAD = 8
CONV_WIDTH = 3
CONV_HALO = 8
N_MOD = 6
EPS = 1e-6

F32 = jnp.float32
BF16 = jnp.bfloat16


def _tile_sizes(seq):
    return min(128, seq), min(512, seq), 256


def _const_spec(shape):
    zeros = (0,) * len(shape)
    return pl.BlockSpec(shape, lambda *_: zeros, pipeline_mode=pl.Buffered(1))


def _rms_norm(x, g):
    ms = jnp.mean(x * x, axis=-1, keepdims=True)
    return x * lax.rsqrt(ms + EPS) * g


def _zero_row(width, *arrays):
    acc = None
    for a in arrays:
        bits = lax.bitcast_convert_type(a.astype(F32), jnp.int32)
        rows, cols = bits.shape
        blocks = [bits[r:r + V7X_SUBLANES] for r in range(0, rows, V7X_SUBLANES)]
        bits = functools.reduce(jnp.bitwise_or, blocks)
        tiles = [bits[:, c:c + V7X_LANES] for c in range(0, cols, V7X_LANES)]
        bits = functools.reduce(jnp.bitwise_or, tiles)
        acc = bits if acc is None else acc | bits
    zero = lax.shift_right_logical(lax.shift_right_logical(acc, 16), 16).astype(F32)
    return jnp.concatenate([zero[0:1]] * (width // V7X_LANES), axis=1)


def _adaln_kernel(c_ref, w_ref, b_ref, o_ref):
    c = c_ref[...]
    s = c * jax.nn.sigmoid(c)
    o_ref[...] = jnp.dot(s.astype(BF16), w_ref[...].astype(BF16),
                         preferred_element_type=F32) + b_ref[...]


def _adaln(c, w_ada, b_ada):
    bsz, d = c.shape
    n = w_ada.shape[1]
    return pl.pallas_call(
        _adaln_kernel,
        grid=(n // d,),
        in_specs=[
            pl.BlockSpec((bsz, d), lambda k: (0, 0)),
            pl.BlockSpec((d, d), lambda k: (0, k)),
            pl.BlockSpec((1, d), lambda k: (0, k)),
        ],
        out_specs=pl.BlockSpec((bsz, d), lambda k: (0, k)),
        out_shape=jax.ShapeDtypeStruct((bsz, n), F32),
        name="adaln",
    )(c, w_ada, b_ada.reshape(1, n))


def _gelu_tanh(x):
    return 0.5 * x * (1.0 + jnp.tanh(0.7978845608028654 * (x + 0.044715 * (x * x * x))))


def _granule_transpose(v):
    granule = lax.broadcasted_iota(jnp.int32, v[0].shape, 1) // SSM_GROUP
    v = list(v)
    for d in (4, 2, 1):
        hi = (granule & d) != 0
        nxt = list(v)
        for i in range(len(v)):
            if i & d:
                continue
            lo_v, hi_v = v[i], v[i + d]
            nxt[i] = jnp.where(hi, pltpu.roll(hi_v, SSM_GROUP * d, axis=1), lo_v)
            nxt[i + d] = jnp.where(hi, hi_v, pltpu.roll(lo_v, V7X_LANES - SSM_GROUP * d, axis=1))
        v = nxt
    return v


def _mixer_kernel(x_ref, sh_ref, sc_ref, gt_ref, gmix_ref, win_ref, toep_ref, ba_ref, ca_ref, lam_ref,
                  wglu_ref, bglu_ref, wpool_ref, bpool_ref, pscale_ref, wout_ref, wup_ref, wdown_ref,
                  o_ref, wup_bf_ref, wdown_bf_ref, utm_ref, ytm_ref, yblk_ref, state_ref, halo_ref, lvl_ref, ublk0, ublk1, w0, w1, pbuf0,
                  pbuf1,
                  *, tt, n_tiles, n_total):
    s = pl.program_id(0)
    first_front = jnp.minimum(s, n_total - 1) % n_tiles == 0
    i_back = jnp.maximum(s - 1, 0) % n_tiles
    first_back = i_back == 0
    nb, _, d = x_ref.shape
    n_slab, n_pair = toep_ref.shape[0], toep_ref.shape[1]
    ssm_w = n_slab * V7X_LANES
    pool_w = halo_ref.shape[2]
    pgw = pool_w // len(POOL_WINDOWS)
    nk = tt // SSM_BLOCK
    m = nk * nb
    tile = 2 * V7X_LANES

    wup_bf_ref[...] = wup_ref[...].astype(BF16)
    wdown_bf_ref[...] = wdown_ref[...].astype(BF16)

    @pl.when(s == 0)
    def _():
        ublk1[...] = jnp.zeros_like(ublk1)
        w1[...] = jnp.zeros_like(w1)
        pbuf1[...] = jnp.zeros_like(pbuf1)
        pbuf0[:, 0:POOL_PAD, :] = jnp.zeros((nb, POOL_PAD, pool_w), F32)
        lvl_ref[...] = jnp.zeros_like(lvl_ref)
        halo_ref[...] = jnp.zeros_like(halo_ref)
        state_ref[...] = jnp.zeros_like(state_ref)

    def step(ublk_f, w_f, pbuf_f, ublk_b, w_b, pbuf_b):
        half_d = d // 2

        yblk = []
        for j in range(n_slab):
            for pr in range(n_pair):
                cols = slice(pr * tile, (pr + 1) * tile)
                yb = jnp.dot(ublk_b[j, :, cols], toep_ref[j, pr], preferred_element_type=F32)
                yblk.append(yb + jnp.dot(w_b[j, :, cols].astype(BF16), ca_ref[j, pr],
                                         preferred_element_type=F32))
        h = _rms_norm(x_ref[...], gmix_ref[...])
        h = h * (1.0 + sc_ref[...][:, None, :]) + sh_ref[...][:, None, :]
        hb = h.reshape(nb * tt, d).astype(BF16)
        for j in range(n_slab):
            for pr in range(n_pair):
                yblk_ref[j, :, pr * tile:(pr + 1) * tile] = yblk[j * n_pair + pr]

        z_u = jnp.dot(hb, win_ref[:, :ssm_w], preferred_element_type=F32)
        for j in range(n_slab):
            groups = [yblk_ref[j, :, g * V7X_LANES:(g + 1) * V7X_LANES] for g in range(2 * n_pair)]
            steps = _granule_transpose(groups)
            y4 = jnp.stack([v.reshape(nk, nb, V7X_LANES) for v in steps], axis=1)
            ytm_ref[j] = y4.reshape(tt * nb, V7X_LANES)
        for b in range(nb):
            for j in range(n_slab):
                utm_ref[j, pl.ds(b, tt, stride=nb), :] = (
                    z_u[b * tt:(b + 1) * tt, j * V7X_LANES:(j + 1) * V7X_LANES])

        z_p = jnp.dot(hb, win_ref[:, ssm_w:], preferred_element_type=F32)
        y_nat = jnp.concatenate(
            [jnp.concatenate([ytm_ref[j, pl.ds(b, tt, stride=nb), :] for j in range(n_slab)], axis=1)
             for b in range(nb)], axis=0)
        yg = _gelu_tanh(y_nat).astype(BF16)
        pbuf_f[:, POOL_PAD + POOL_HALO:, :] = z_p.reshape(nb, tt, pool_w)

        lo, hi = POOL_PAD, POOL_PAD + POOL_HALO + tt
        pbuf_b[:, lo:lo + POOL_HALO, :] = jnp.where(first_back, 0.0, halo_ref[...])
        halo_ref[...] = pbuf_b[:, hi - POOL_HALO:hi, :]
        pos = (i_back * tt + 1 + lax.broadcasted_iota(jnp.int32, (1, tt, 1), 1)).astype(F32)

        def pooled(gi):
            win = POOL_WINDOWS[gi]
            cols = slice(gi * pgw, (gi + 1) * pgw)
            tok = pbuf_b[:, lo + POOL_HALO:hi, cols]
            sums = pbuf_b[:, lo:hi, cols] + pbuf_b[:, lo - 1:hi - 1, cols]
            span, level = 2, 0
            while span < win:
                lv = lvl_ref.at[level % 2]
                lv[:, lo:hi, :] = sums
                sums = sums + lv[:, lo - span:hi - span, :]
                span, level = 2 * span, level + 1
            zc = sums[:, POOL_HALO:, :] * (1.0 / jnp.minimum(pos, float(win))) - tok
            return zc.reshape(nb * tt, pgw).astype(BF16)

        n_pg = len(POOL_WINDOWS)
        t_val = jnp.dot(yg, wglu_ref[:, :ssm_w], preferred_element_type=F32)
        zc = [pooled(gi) for gi in range(n_pg // 2)]
        t_gate = jnp.dot(yg, wglu_ref[:, ssm_w:], preferred_element_type=F32)
        zc += [pooled(gi) for gi in range(n_pg // 2, n_pg)]
        y_ssm = ((t_val + bglu_ref[:, :ssm_w])
                 * jax.nn.sigmoid(t_gate + bglu_ref[:, ssm_w:])).astype(BF16)

        for c in range(2):
            cs = slice(c * half_d, (c + 1) * half_d)
            o_ref[:, :, cs] = jnp.dot(y_ssm, wout_ref[:ssm_w, cs],
                                      preferred_element_type=F32).reshape(nb, tt, half_d)

        ubs = []
        for j in range(n_slab):
            u4 = utm_ref[j].reshape(nk, SSM_BLOCK, nb, V7X_LANES)
            steps = [u4[:, q].reshape(m, V7X_LANES) for q in range(SSM_BLOCK)]
            ub = jnp.concatenate(_granule_transpose(steps), axis=1).astype(BF16)
            ublk_f[j] = ub
            ubs.append(ub)
        for j in range(n_slab):
            for pr in range(n_pair):
                cols = slice(pr * tile, (pr + 1) * tile)
                w_f[j, :, cols] = jnp.dot(ubs[j][:, cols], ba_ref[j, pr], preferred_element_type=F32)
        y_pool = []
        for gi in range(n_pg):
            cols = slice(gi * pgw, (gi + 1) * pgw)
            zp = jnp.dot(zc[gi], wpool_ref[gi], preferred_element_type=F32)
            y_pool.append((zp + bpool_ref[:, cols]) * pscale_ref[:, cols])
        y_pool = jnp.concatenate(y_pool, axis=-1).astype(BF16)

        chains = [(j, pr, g) for j in range(n_slab) for pr in range(n_pair) for g in range(2)]
        state = [jnp.where(first_front, 0.0, state_ref[j, pr, g]) for j, pr, g in chains]
        for k in range(nk):
            rows = slice(k * nb, (k + 1) * nb)
            for ci, (j, pr, g) in enumerate(chains):
                cols = slice(pr * tile + g * V7X_LANES, pr * tile + (g + 1) * V7X_LANES)
                x_prev = state[ci]
                w_k = w_f[j, rows, cols]
                w_f[j, rows, cols] = x_prev
                state[ci] = (lam_ref[j, pr, g, 0] * x_prev
                             + lam_ref[j, pr, g, 1] * pltpu.roll(x_prev, V7X_LANES // 2, axis=1) + w_k)
        for (j, pr, g), x_last in zip(chains, state):
            state_ref[j, pr, g] = x_last

        for c in range(2):
            cs = slice(c * half_d, (c + 1) * half_d)
            mixed = jnp.dot(y_pool, wout_ref[ssm_w:, cs], preferred_element_type=F32)
            o_ref[:, :, cs] = gt_ref[:, cs][:, None, :] * (o_ref[:, :, cs]
                                                           + mixed.reshape(nb, tt, half_d))

    @pl.when(s % 2 == 0)
    def _():
        step(ublk0, w0, pbuf0, ublk1, w1, pbuf1)

    @pl.when(s % 2 == 1)
    def _():
        step(ublk1, w1, pbuf1, ublk0, w0, pbuf0)


def _cast_block_rows(rows, n_steps):
    tile = 2 * V7X_SUBLANES
    return next(r for r in range(tile, rows + 1, tile) if rows % r == 0 and rows // r <= n_steps)


def _mixer(x, mod, g_norm, w_in_bf, toep, ba, ca, lam_t, w_glu_bf, b_glu, w_pool_bf, b_pool, pool_scale,
           w_out_bf, w_up, w_down, *, tt):
    bsz, seq, d = x.shape
    nb = BATCH_GROUP
    n_slab, n_pair = toep.shape[0], toep.shape[1]
    ssm_w = n_slab * V7X_LANES
    pool_w = w_in_bf.shape[1] - ssm_w
    n_tiles = seq // tt
    n_total = (bsz // nb) * n_tiles
    m = tt // SSM_BLOCK * nb
    width = n_pair * 2 * V7X_LANES

    def front(s):
        t = jnp.minimum(s, n_total - 1)
        return t // n_tiles, t % n_tiles

    def back(s):
        t = jnp.maximum(s - 1, 0)
        return t // n_tiles, t % n_tiles

    up_rows = _cast_block_rows(w_up.shape[0], n_total)
    down_rows = _cast_block_rows(w_down.shape[0], n_total)
    up_spec = pl.BlockSpec((up_rows, w_up.shape[1]),
                           lambda s: (jnp.minimum(s, w_up.shape[0] // up_rows - 1), 0))
    down_spec = pl.BlockSpec((down_rows, w_down.shape[1]),
                             lambda s: (jnp.minimum(s, w_down.shape[0] // down_rows - 1), 0))

    kern = functools.partial(_mixer_kernel, tt=tt, n_tiles=n_tiles, n_total=n_total)
    return pl.pallas_call(
        kern,
        grid=(n_total + 1,),
        in_specs=[
            pl.BlockSpec((nb, tt, d), lambda s: (*front(s), 0)),
            pl.BlockSpec((nb, d), lambda s: (front(s)[0], 0)),
            pl.BlockSpec((nb, d), lambda s: (front(s)[0], 1)),
            pl.BlockSpec((nb, d), lambda s: (back(s)[0], 2)),
            _const_spec((1, d)),
            _const_spec(w_in_bf.shape),
            _const_spec(toep.shape),
            _const_spec(ba.shape),
            _const_spec(ca.shape),
            _const_spec(lam_t.shape),
            _const_spec(w_glu_bf.shape),
            _const_spec((1, 2 * ssm_w)),
            _const_spec(w_pool_bf.shape),
            _const_spec((1, pool_w)),
            _const_spec((1, pool_w)),
            _const_spec(w_out_bf.shape),
            up_spec,
            down_spec,
        ],
        out_specs=[pl.BlockSpec((nb, tt, d), lambda s: (*back(s), 0)), up_spec, down_spec],
        out_shape=[jax.ShapeDtypeStruct(x.shape, F32), jax.ShapeDtypeStruct(w_up.shape, BF16),
                   jax.ShapeDtypeStruct(w_down.shape, BF16)],
        scratch_shapes=[
            pltpu.VMEM((n_slab, tt * nb, V7X_LANES), F32),
            pltpu.VMEM((n_slab, tt * nb, V7X_LANES), F32),
            pltpu.VMEM((n_slab, m, width), F32),
            pltpu.VMEM((n_slab, n_pair, 2, nb, V7X_LANES), F32),
            pltpu.VMEM((nb, POOL_HALO, pool_w), F32),
            pltpu.VMEM((2, nb, POOL_PAD + POOL_HALO + tt, pool_w // len(POOL_WINDOWS)), F32),
            pltpu.VMEM((n_slab, m, width), BF16),
            pltpu.VMEM((n_slab, m, width), BF16),
            pltpu.VMEM((n_slab, m, width), F32),
            pltpu.VMEM((n_slab, m, width), F32),
            pltpu.VMEM((nb, POOL_PAD + POOL_HALO + tt, pool_w), F32),
            pltpu.VMEM((nb, POOL_PAD + POOL_HALO + tt, pool_w), F32),
        ],
        compiler_params=pltpu.CompilerParams(
            dimension_semantics=("arbitrary",),
            vmem_limit_bytes=V7X_VMEM_LIMIT_BYTES),
        name="mixer",
    )(x, mod, mod, mod, g_norm.reshape(1, d), w_in_bf, toep, ba, ca, lam_t, w_glu_bf,
      b_glu.reshape(1, -1), w_pool_bf, b_pool.reshape(1, -1), pool_scale.reshape(1, -1), w_out_bf,
      w_up, w_down)


FFN_ROW_SLICES = 8
FFN_UP_AHEAD = 3


def _ffn_kernel(xf_ref, dlf_ref, sh_ref, sc_ref, xb_ref, dlb_ref, gt_ref, gn_ref, wup_ref, wconv_ref,
                bconv_ref, wdown_ref, gfin_ref, o_ref, gbuf_ref, gprev_ref, h0, h1, acc0, acc1,
                *, tm, fc, n_tiles):
    s = pl.program_id(0)
    first_mid = jnp.maximum(s - 1, 0) % n_tiles == 0
    d_ff = wdown_ref.shape[0]
    n_chunks = d_ff // fc
    n_gbuf = gbuf_ref.shape[0]
    rs = tm // FFN_ROW_SLICES

    @pl.when(s == 0)
    def _():
        h1[...] = jnp.zeros_like(h1)
        acc0[...] = jnp.zeros_like(acc0)
        gprev_ref[...] = jnp.zeros_like(gprev_ref)

    def step(h_f, h_m, acc_m, acc_b):
        def up(c):
            cols = slice(c * fc, (c + 1) * fc)
            gb = gbuf_ref.at[c % n_gbuf]
            h = h_m[...]
            v = jnp.dot(h, wup_ref[:, c * fc:(c + 1) * fc], preferred_element_type=F32)
            g = jnp.dot(h, wup_ref[:, d_ff + c * fc:d_ff + (c + 1) * fc],
                        preferred_element_type=F32)
            gb[0:CONV_HALO, :] = jnp.where(first_mid, 0.0, gprev_ref[:, cols])
            gb[CONV_HALO:, :] = g
            gprev_ref[:, cols] = g[tm - CONV_HALO:, :]
            return v

        pending = {c: up(c) for c in range(FFN_UP_AHEAD)}
        anchor = None
        for c in range(n_chunks):
            cols = slice(c * fc, (c + 1) * fc)
            if c + FFN_UP_AHEAD < n_chunks:
                pending[c + FFN_UP_AHEAD] = up(c + FFN_UP_AHEAD)
            v = pending.pop(c)
            bias = bconv_ref[:, cols]
            if anchor is not None:
                bias = bias + anchor
            gb = gbuf_ref.at[c % n_gbuf]
            gc = bias + wconv_ref[CONV_WIDTH - 1:CONV_WIDTH, cols] * gb[CONV_HALO:, :]
            for k in range(1, CONV_WIDTH):
                gc = gc + (wconv_ref[CONV_WIDTH - 1 - k:CONV_WIDTH - k, cols]
                           * gb[CONV_HALO - k:CONV_HALO - k + tm, :])
            a = (gc * jax.nn.sigmoid(gc) * v).astype(BF16)
            if c == 0:
                acc_m[...] = jnp.dot(a, wdown_ref[cols, :], preferred_element_type=F32)
            elif c < n_chunks - 1:
                acc_m[...] += jnp.dot(a, wdown_ref[cols, :], preferred_element_type=F32)
            else:
                for r in range(2):
                    rows = slice(r * (tm // 2), (r + 1) * (tm // 2))
                    acc_m[rows, :] += jnp.dot(a[rows, :], wdown_ref[cols, :],
                                              preferred_element_type=F32)
            anchor = None
            if c < FFN_ROW_SLICES:
                rows = slice(c * rs, (c + 1) * rs)
                y = xb_ref[0, rows, :] + dlb_ref[0, rows, :] + gt_ref[0, 0] * acc_b[rows, :]
                y = _rms_norm(y, gfin_ref[...])
                o_ref[0, rows, :] = y
                hf = _rms_norm(xf_ref[0, rows, :] + dlf_ref[0, rows, :], gn_ref[...])
                hf = hf * (1.0 + sc_ref[0, 0]) + sh_ref[0, 0]
                h_f[rows, :] = hf.astype(BF16)
                anchor = _zero_row(fc, y, hf)

    @pl.when(s % 2 == 0)
    def _():
        step(h0, h1, acc1, acc0)

    @pl.when(s % 2 == 1)
    def _():
        step(h1, h0, acc0, acc1)


def _ffn(x, delta, mod, g_norm, w_up_bf, w_conv, b_conv, w_down_bf, g_final, *, tm, fc):
    bsz, seq, d = x.shape
    d_ff = w_down_bf.shape[0]
    n_tiles = seq // tm
    n_total = bsz * n_tiles
    assert d_ff % fc == 0 and d_ff // fc >= FFN_ROW_SLICES and tm % (FFN_ROW_SLICES * V7X_SUBLANES) == 0
    mod4 = mod.reshape(bsz, N_MOD, 1, d)

    def front(s):
        t = jnp.minimum(s, n_total - 1)
        return t // n_tiles, t % n_tiles

    def back(s):
        t = jnp.maximum(s - 2, 0)
        return t // n_tiles, t % n_tiles

    kern = functools.partial(_ffn_kernel, tm=tm, fc=fc, n_tiles=n_tiles)
    return pl.pallas_call(
        kern,
        grid=(n_total + 2,),
        in_specs=[
            pl.BlockSpec((1, tm, d), lambda s: (*front(s), 0)),
            pl.BlockSpec((1, tm, d), lambda s: (*front(s), 0)),
            pl.BlockSpec((1, 1, 1, d), lambda s: (front(s)[0], 3, 0, 0)),
            pl.BlockSpec((1, 1, 1, d), lambda s: (front(s)[0], 4, 0, 0)),
            pl.BlockSpec((1, tm, d), lambda s: (*back(s), 0)),
            pl.BlockSpec((1, tm, d), lambda s: (*back(s), 0)),
            pl.BlockSpec((1, 1, 1, d), lambda s: (back(s)[0], 5, 0, 0)),
            _const_spec((1, d)),
            _const_spec(w_up_bf.shape),
            _const_spec(w_conv.shape),
            _const_spec((1, d_ff)),
            _const_spec(w_down_bf.shape),
            _const_spec((1, d)),
        ],
        out_specs=pl.BlockSpec((1, tm, d), lambda s: (*back(s), 0)),
        out_shape=jax.ShapeDtypeStruct(x.shape, F32),
        scratch_shapes=[
            pltpu.VMEM((FFN_UP_AHEAD + 1, CONV_HALO + tm, fc), F32),
            pltpu.VMEM((CONV_HALO, d_ff), F32),
            pltpu.VMEM((tm, d), BF16),
            pltpu.VMEM((tm, d), BF16),
            pltpu.VMEM((tm, d), F32),
            pltpu.VMEM((tm, d), F32),
        ],
        compiler_params=pltpu.CompilerParams(
            dimension_semantics=("arbitrary",),
            vmem_limit_bytes=V7X_VMEM_LIMIT_BYTES),
        name="ffn",
    )(x, delta, mod4, mod4, x, delta, mod4, g_norm.reshape(1, d), w_up_bf, w_conv,
      b_conv.reshape(1, d_ff), w_down_bf, g_final.reshape(1, d))


def _ssm_params(lam_re, lam_im, log_dt, b_re, b_im, c_re, c_im, d_skip):
    n_groups, n_state = lam_re.shape
    h, t = SSM_GROUP, SSM_BLOCK
    th = t * h
    gps = V7X_LANES // h
    n_slab, n_pair = n_groups // gps, gps // 2
    lam_re, lam_im = lam_re.astype(F32), lam_im.astype(F32)
    dt = jnp.exp(log_dt.astype(F32))[:, None]

    k = jnp.arange(t + 1, dtype=F32)[None, :, None]
    mag = jnp.exp(k * (dt * lam_re)[:, None, :])
    ang = k * (dt * lam_im)[:, None, :]
    p_re, p_im = mag * jnp.cos(ang), mag * jnp.sin(ang)

    a_re, a_im = p_re[:, 1], p_im[:, 1]
    den = lam_re * lam_re + lam_im * lam_im
    q_re = (((a_re - 1.0) * lam_re + a_im * lam_im) / den)[:, None, :]
    q_im = ((a_im * lam_re - (a_re - 1.0) * lam_im) / den)[:, None, :]
    b_re = b_re.astype(F32).transpose(0, 2, 1)
    b_im = b_im.astype(F32).transpose(0, 2, 1)
    bb_re = q_re * b_re - q_im * b_im
    bb_im = q_re * b_im + q_im * b_re
    c_re, c_im = c_re.astype(F32)[:, None], c_im.astype(F32)[:, None]

    pk_re, pk_im = p_re[:, :, None, :], p_im[:, :, None, :]
    cak = jnp.concatenate([c_re * pk_re - c_im * pk_im, -(c_re * pk_im + c_im * pk_re)], axis=-1)
    cat0 = cak[:, :t].reshape(n_groups, th, 2 * n_state)
    cat = cak[:, 1:].reshape(n_groups, th, 2 * n_state)
    r_re, r_im = p_re[:, t - 1::-1][:, :, None, :], p_im[:, t - 1::-1][:, :, None, :]
    ba = jnp.concatenate([r_re * bb_re[:, None] - r_im * bb_im[:, None],
                          r_re * bb_im[:, None] + r_im * bb_re[:, None]], axis=-1)
    ba = ba.reshape(n_groups, th, 2 * n_state)
    bmat = jnp.concatenate([bb_re, bb_im], axis=-1)
    kexp = jnp.einsum('ghk,gnk->ghn', bmat, cat0)
    kexp = kexp + d_skip.astype(F32).reshape(n_groups, h)[:, :, None] * jnp.eye(h, th, dtype=F32)
    toep = jnp.stack([jnp.pad(kexp, ((0, 0), (0, 0), (h * si, 0)))[:, :, :th] for si in range(t)],
                     axis=1).reshape(n_groups, th, th)

    def pair_diag(m):
        m = m.reshape(n_slab, n_pair, 2, th, th)
        z = jnp.zeros_like(m[:, :, 0])
        return jnp.concatenate([jnp.concatenate([m[:, :, 0], z], axis=-1),
                                jnp.concatenate([z, m[:, :, 1]], axis=-1)], axis=-2)

    lam_t = jnp.stack([jnp.concatenate([p_re[:, t], p_re[:, t]], axis=-1),
                       jnp.concatenate([-p_im[:, t], p_im[:, t]], axis=-1)], axis=1)
    lam_t = jnp.broadcast_to(lam_t.reshape(n_slab, n_pair, 2, 2, 1, 2 * n_state),
                             (n_slab, n_pair, 2, 2, BATCH_GROUP, 2 * n_state))
    ca = pair_diag(cat).astype(BF16).swapaxes(2, 3)
    return pair_diag(toep).astype(BF16), pair_diag(ba).astype(BF16), ca, lam_t


def _layer(x, c, w_ada, b_ada, g_norm_mix, w_in, ssm_lam_re, ssm_lam_im, ssm_log_dt, ssm_b_re,
           ssm_b_im, ssm_c_re, ssm_c_im, ssm_d, w_glu, b_glu, w_pool, b_pool, pool_scale, w_out,
           g_norm_ffn, w_up, w_conv, b_conv, w_down, g_final):
    bsz, seq, d = x.shape
    ssm_w = ssm_d.shape[0]
    tt, tm_ffn, fc = _tile_sizes(seq)
    assert bsz % BATCH_GROUP == 0 and ssm_w % V7X_LANES == 0
    assert seq % tt == 0 and seq % tm_ffn == 0 and tt >= POOL_HALO and tt % SSM_BLOCK == 0

    mod = _adaln(c, w_ada, b_ada)
    toep, ba, ca, lam_t = _ssm_params(ssm_lam_re, ssm_lam_im, ssm_log_dt, ssm_b_re, ssm_b_im,
                                      ssm_c_re, ssm_c_im, ssm_d)
    delta, w_up_bf, w_down_bf = _mixer(
        x, mod, g_norm_mix, w_in.astype(BF16), toep, ba, ca, lam_t, w_glu.astype(BF16), b_glu,
        w_pool.astype(BF16), b_pool, pool_scale, w_out.astype(BF16), w_up, w_down, tt=tt)
    return _ffn(x, delta, mod, g_norm_ffn, w_up_bf, w_conv, b_conv, w_down_bf, g_final,
                tm=tm_ffn, fc=fc)


def kernel(x, c, w_ada, b_ada, g_norm_mix, w_in, ssm_lam_re, ssm_lam_im, ssm_log_dt, ssm_b_re,
           ssm_b_im, ssm_c_re, ssm_c_im, ssm_d, w_glu, b_glu, w_pool, b_pool, pool_scale, w_out,
           g_norm_ffn, w_up, w_conv, b_conv, w_down, g_norm_final):
    assert w_ada.shape[0] == 1, "single-layer block"
    return _layer(x, c, w_ada[0], b_ada[0], g_norm_mix[0], w_in[0], ssm_lam_re[0], ssm_lam_im[0],
                  ssm_log_dt[0], ssm_b_re[0], ssm_b_im[0], ssm_c_re[0], ssm_c_im[0], ssm_d[0],
                  w_glu[0], b_glu[0], w_pool[0], b_pool[0], pool_scale[0], w_out[0],
                  g_norm_ffn[0], w_up[0], w_conv[0], b_conv[0], w_down[0], g_norm_final)
```
